```python
import jax, jax.numpy as jnp
from jax import lax
import numpy as np

D_MODEL = 1024
BATCH = 8
SEQ = 2048
DEPTH = 4

CHUNK = 128
A_WIDTH = D_MODEL
A_GROUPS = 8
A_GROUP_DIM = A_WIDTH // A_GROUPS
N_HEADS = 16
HEAD_DIM = 64
B_WIDTH = N_HEADS * HEAD_DIM
DILATED_PATTERNS = ((128, 1), (512, 4), (2048, 16))
BLOCK = 128
N_BRANCHES = 2
EPS = 1e-6
NEG_INF = -1e30
IN_COLS = 3 * A_WIDTH + 4 * B_WIDTH + N_BRANCHES * D_MODEL

kernel_name = "hybrid_gmlp_dilated_attn_block"


def rmsnorm(x, g):
    xf = x.astype(jnp.float32)
    y = xf * lax.rsqrt(jnp.mean(xf * xf, axis=-1, keepdims=True) + EPS)
    return (y * g.astype(jnp.float32)).astype(x.dtype)


def chunked_spatial_gating(u, v, w_s, b_s, g_v):
    Bn, S, _ = v.shape
    v = rmsnorm(v, g_v)
    vc = v.reshape(Bn, S // CHUNK, CHUNK, A_GROUPS, A_GROUP_DIM)
    causal = jnp.tril(jnp.ones((CHUNK, CHUNK), dtype=bool))
    w = jnp.where(causal[None], w_s, 0).astype(v.dtype)
    mixed = jnp.einsum('gts,bcsgd->bctgd', w, vc) + b_s.T[None, None, :, :, None]
    return u * mixed.reshape(Bn, S, A_WIDTH)


def dilated_pattern(q, k, v, slopes, window, dilation):
    Bn, H, S, Dh = q.shape
    L = S // dilation
    nback = window // dilation
    nb = -(-L // BLOCK)
    pad = nb * BLOCK - L

    def to_blocks(t):
        t = t.reshape(Bn, H, L, dilation, Dh).transpose(0, 1, 3, 2, 4)
        t = jnp.pad(t, ((0, 0), (0, 0), (0, 0), (0, pad), (0, 0)))
        return t.reshape(Bn, H, dilation, nb, BLOCK, Dh)

    qb, kb, vb = to_blocks(q), to_blocks(k), to_blocks(v)

    def with_prev(t):
        prev = jnp.pad(t, ((0, 0), (0, 0), (0, 0), (1, 0), (0, 0), (0, 0)))[:, :, :, :-1]
        return jnp.concatenate([prev, t], axis=4)

    kw, vw = with_prev(kb), with_prev(vb)
    qpos = jnp.arange(BLOCK)[:, None] + BLOCK
    kpos = jnp.arange(2 * BLOCK)[None, :]
    dist = qpos - kpos
    key_idx = (jnp.arange(nb)[:, None, None] - 1) * BLOCK + kpos[None]
    valid = (dist >= 0) & (dist <= nback) & (key_idx >= 0)

    s = jnp.einsum('bhrnqd,bhrnkd->bhrnqk', qb, kw) * (Dh ** -0.5)
    s = s - slopes[None, :, None, None, None, None] * (dist * dilation).astype(jnp.float32)
    s = jnp.where(valid[None, None, None], s, NEG_INF)
    m = jnp.max(s, axis=-1)
    p = jnp.exp(s - m[..., None])
    l = jnp.sum(p, axis=-1)
    o = jnp.einsum('bhrnqk,bhrnkd->bhrnqd', p, vw) / l[..., None]

    def from_blocks(t):
        t = t.reshape(Bn, H, dilation, nb * BLOCK, *t.shape[5:])[:, :, :, :L]
        t = jnp.moveaxis(t, 2, 3)
        return t.reshape(Bn, H, S, *t.shape[4:])

    return from_blocks(o), from_blocks(m), from_blocks(l)


def dilated_attention(q, k, v):
    Bn, S, _ = q.shape
    heads = lambda t: t.astype(jnp.float32).reshape(Bn, S, N_HEADS, HEAD_DIM).transpose(0, 2, 1, 3)
    qh, kh, vh = heads(q), heads(k), heads(v)
    slopes = 2.0 ** (-8.0 * jnp.arange(1, N_HEADS + 1, dtype=jnp.float32) / N_HEADS)
    outs = [dilated_pattern(qh, kh, vh, slopes, w, d) for (w, d) in DILATED_PATTERNS]
    m_all = jnp.max(jnp.stack([m for (_, m, _) in outs]), axis=0)
    alphas = [l * jnp.exp(m - m_all) for (_, m, l) in outs]
    num = sum(a[..., None] * o for a, (o, _, _) in zip(alphas, outs))
    o = num / sum(alphas)[..., None]
    return o.transpose(0, 2, 1, 3).reshape(Bn, S, B_WIDTH).astype(q.dtype)


def setup_inputs(seed: int = 0) -> dict:
    key = jax.random.key(seed)
    ks = jax.random.split(key, 10)
    f32 = jnp.float32
    x = jax.random.normal(ks[0], (BATCH, SEQ, D_MODEL), f32)
    g_norm = 1.0 + 0.05 * jax.random.normal(ks[1], (DEPTH, D_MODEL), f32)
    w_in = jax.random.normal(ks[2], (DEPTH, D_MODEL, IN_COLS), f32) * D_MODEL ** -0.5
    w_s = jax.random.normal(ks[3], (DEPTH, A_GROUPS, CHUNK, CHUNK), f32) * CHUNK ** -0.5
    b_s = 1.0 + 0.1 * jax.random.normal(ks[4], (DEPTH, A_GROUPS, CHUNK), f32)
    g_v = 1.0 + 0.05 * jax.random.normal(ks[5], (DEPTH, A_WIDTH), f32)
    w_proj_a = jax.random.normal(ks[6], (DEPTH, A_WIDTH, D_MODEL), f32) * A_WIDTH ** -0.5
    w_proj_b = jax.random.normal(ks[7], (DEPTH, B_WIDTH, D_MODEL), f32) * B_WIDTH ** -0.5
    w_out = jax.random.normal(ks[8], (DEPTH, D_MODEL, D_MODEL), f32) * D_MODEL ** -0.5
    g_final = 1.0 + 0.05 * jax.random.normal(ks[9], (D_MODEL,), f32)
    return {"x": x, "g_norm": g_norm, "w_in": w_in, "w_s": w_s, "b_s": b_s, "g_v": g_v,
            "w_proj_a": w_proj_a, "w_proj_b": w_proj_b, "w_out": w_out, "g_final": g_final}


def reference(x, g_norm, w_in, w_s, b_s, g_v, w_proj_a, w_proj_b, w_out, g_final):
    splits = [A_WIDTH, 2 * A_WIDTH, 3 * A_WIDTH,
              3 * A_WIDTH + B_WIDTH, 3 * A_WIDTH + 2 * B_WIDTH,
              3 * A_WIDTH + 3 * B_WIDTH, 3 * A_WIDTH + 4 * B_WIDTH]
    for layer in range(DEPTH):
        h = rmsnorm(x, g_norm[layer])
        proj = h @ w_in[layer]
        a_u, a_v, a_gate, q, k, v, b_gate, gate_logits = jnp.split(proj, splits, axis=-1)
        y_a = chunked_spatial_gating(jax.nn.gelu(a_u), jax.nn.gelu(a_v),
                                     w_s[layer], b_s[layer], g_v[layer]) * jax.nn.silu(a_gate)
        y_b = dilated_attention(q, k, v) * jax.nn.silu(b_gate)
        g_a, g_b = jnp.split(jax.nn.sigmoid(gate_logits), N_BRANCHES, axis=-1)
        merged = g_a * (y_a @ w_proj_a[layer]) + g_b * (y_b @ w_proj_b[layer])
        x = x + merged @ w_out[layer]
    return rmsnorm(x, g_final)
```

```python
import functools
import math

import numpy as np
import jax
import jax.numpy as jnp
from jax import lax
from jax.experimental import pallas as pl
from jax.experimental.pallas import tpu as pltpu

D_MODEL = 1024
DEPTH = 4
CHUNK = 128
A_GROUPS = 8
N_HEADS = 16
HEAD_DIM = 64
DILATIONS = (1, 4, 16)
BLOCK = 128
EPS = 1e-6
NEG_INF = -1e30
N_COL_BLOCKS = 9
IN_COLS = N_COL_BLOCKS * D_MODEL

LANES = 128
VMEM_LIMIT = 56 * 1024 * 1024

TM_IN = 1024
TM_MIX = 512

F32 = jnp.float32
BF16 = jnp.bfloat16


_K1 = 2.0 * math.sqrt(2.0 / math.pi)
_K3 = _K1 * 0.044715
_GELU = (1.0, 0.0, 0.0, 1.0, _K1, _K3)
_SILU = (1.0, 0.0, 0.0, 1.0, 1.0, 0.0)
_SIGM = (0.0, 1.0, 0.0, 1.0, 1.0, 0.0)
_IDEN = (1.0, 0.0, 1.0, 0.0, 0.0, 0.0)
_QSCL = (HEAD_DIM ** -0.5, 0.0, 1.0, 0.0, 0.0, 0.0)
_COEF = np.array([_GELU, _GELU, _SILU, _QSCL, _IDEN, _IDEN, _SILU, _SIGM, _SIGM],
                 dtype=np.float32)


def _inproj_kernel(coef_ref, x_ref, g_ref, w_ref, o_ref, h_ref):
    j = pl.program_id(1)

    @pl.when(j == 0)
    def _():
        x = x_ref[...]
        ms = jnp.mean(x * x, axis=-1, keepdims=True)
        h_ref[...] = (x * lax.rsqrt(ms + EPS) * g_ref[...]).astype(BF16)

    y = jnp.dot(h_ref[...], w_ref[...], preferred_element_type=F32)
    a = coef_ref[j, 0]
    b = coef_ref[j, 1]
    c = coef_ref[j, 2]
    d = coef_ref[j, 3]
    k1 = coef_ref[j, 4]
    k3 = coef_ref[j, 5]
    z = y * (k1 + k3 * (y * y))
    sig = 1.0 / (1.0 + jnp.exp(-z))
    o_ref[...] = ((a * y + b) * (c + d * sig)).astype(BF16)


def _inproj(x2d, g, w_bf16):
    m = x2d.shape[0]
    grid = (m // TM_IN, N_COL_BLOCKS)
    return pl.pallas_call(
        _inproj_kernel,
        grid_spec=pltpu.PrefetchScalarGridSpec(
            num_scalar_prefetch=1,
            grid=grid,
            in_specs=[
                pl.BlockSpec((TM_IN, D_MODEL), lambda i, j, coef: (i, 0)),
                pl.BlockSpec((1, D_MODEL), lambda i, j, coef: (0, 0)),
                pl.BlockSpec((D_MODEL, D_MODEL), lambda i, j, coef: (0, j)),
            ],
            out_specs=pl.BlockSpec((TM_IN, D_MODEL), lambda i, j, coef: (i, j)),
            scratch_shapes=[pltpu.VMEM((TM_IN, D_MODEL), BF16)],
        ),
        out_shape=jax.ShapeDtypeStruct((m, IN_COLS), BF16),
        compiler_params=pltpu.CompilerParams(
            dimension_semantics=("arbitrary", "arbitrary"),
            vmem_limit_bytes=VMEM_LIMIT),
    )(jnp.asarray(_COEF), x2d, g, w_bf16)


def _attn_bias_tables():
    slopes = 2.0 ** (-8.0 * np.arange(1, N_HEADS + 1, dtype=np.float64) / N_HEADS)
    qpos = np.arange(BLOCK)[:, None] + BLOCK
    kpos = np.arange(2 * BLOCK)[None, :]
    dist = qpos - kpos
    valid = (dist >= 0) & (dist <= BLOCK)
    out = np.zeros((N_HEADS // 2, len(DILATIONS), 2 * BLOCK, 2 * BLOCK), np.float32)
    for hp in range(N_HEADS // 2):
        for p, dil in enumerate(DILATIONS):
            for h in range(2):
                pen = -slopes[2 * hp + h] * (dist * dil)
                out[hp, p, h * BLOCK:(h + 1) * BLOCK] = np.where(valid, pen, NEG_INF)
    return out


def _attn_kernel(q_ref, k_ref, v_ref, bias_ref, o_ref,
                 qf, kf, vf, acc2, l2, m2, acc3, l3, m3):
    seq = q_ref.shape[1]
    qf[...] = q_ref[0].astype(F32)
    kf[...] = k_ref[0].astype(F32)
    vf[...] = v_ref[0].astype(F32)

    lane = lax.broadcasted_iota(jnp.int32, (BLOCK, LANES), 1)
    head0 = lane < HEAD_DIM
    ones_cols = jnp.ones((2 * BLOCK, LANES), BF16)

    def rows(ref, start, size, stride):
        if stride == 1:
            return ref[pl.ds(start, size), :]
        return ref[pl.ds(start, size, stride=stride), :]

    def block(p, qs, ks, nk):
        stride = DILATIONS[p]
        q = rows(qf, qs, BLOCK, stride)
        k = rows(kf, ks, nk, stride).astype(BF16)
        v = rows(vf, ks, nk, stride).astype(BF16)
        q2 = jnp.concatenate([jnp.where(head0, q, 0.0), jnp.where(head0, 0.0, q)],
                             axis=0).astype(BF16)
        s = lax.dot_general(q2, k, (((1,), (1,)), ((), ())),
                            preferred_element_type=F32)
        if nk == 2 * BLOCK:
            s = s + bias_ref[0, p]
        else:
            s = s + bias_ref[0, p, :, BLOCK:]
        m = jnp.max(s, axis=-1, keepdims=True)
        pr = jnp.exp(s - m).astype(BF16)
        vext = jnp.concatenate([v, ones_cols[:nk]], axis=1)
        r = jnp.dot(pr, vext, preferred_element_type=F32)
        acc = jnp.where(head0, r[:BLOCK, :LANES], r[BLOCK:, :LANES])
        l = jnp.where(head0, r[:BLOCK, LANES:], r[BLOCK:, LANES:])
        mb = jnp.where(head0, jnp.broadcast_to(m[:BLOCK], (BLOCK, LANES)),
                       jnp.broadcast_to(m[BLOCK:], (BLOCK, LANES)))
        return acc, l, mb

    def scatter(p, refs, qs, vals):
        stride = DILATIONS[p]
        for ref, val in zip(refs, vals):
            ref[pl.ds(qs, BLOCK, stride=stride), :] = val

    def p3_body(r, carry):
        scatter(2, (acc3, l3, m3), r, block(2, r, r, BLOCK))
        return carry
    lax.fori_loop(0, DILATIONS[2], p3_body, 0)

    d2 = DILATIONS[1]
    nb2 = seq // d2 // BLOCK

    def p2_body(r, carry):
        scatter(1, (acc2, l2, m2), r, block(1, r, r, BLOCK))

        def inner(n, c2):
            qs = r + d2 * BLOCK * n
            ks = r + d2 * BLOCK * (n - 1)
            scatter(1, (acc2, l2, m2), qs, block(1, qs, ks, 2 * BLOCK))
            return c2
        lax.fori_loop(1, nb2, inner, 0)
        return carry
    lax.fori_loop(0, d2, p2_body, 0)

    def finish(qs, vals):
        acc1, l1, m1 = vals
        a2 = acc2[pl.ds(qs, BLOCK), :]
        a3 = acc3[pl.ds(qs, BLOCK), :]
        ll2 = l2[pl.ds(qs, BLOCK), :]
        ll3 = l3[pl.ds(qs, BLOCK), :]
        mm2 = m2[pl.ds(qs, BLOCK), :]
        mm3 = m3[pl.ds(qs, BLOCK), :]
        mx = jnp.maximum(jnp.maximum(m1, mm2), mm3)
        e1 = jnp.exp(m1 - mx)
        e2 = jnp.exp(mm2 - mx)
        e3 = jnp.exp(mm3 - mx)
        num = e1 * acc1 + e2 * a2 + e3 * a3
        den = e1 * l1 + e2 * ll2 + e3 * ll3
        o_ref[0, pl.ds(qs, BLOCK), :] = (num / den).astype(o_ref.dtype)

    finish(0, block(0, 0, 0, BLOCK))

    def p1_body(i, carry):
        qs = pl.multiple_of(i * BLOCK, BLOCK)
        ks = pl.multiple_of((i - 1) * BLOCK, BLOCK)
        finish(qs, block(0, qs, ks, 2 * BLOCK))
        return carry
    lax.fori_loop(1, seq // BLOCK, p1_body, 0)


def _attn(proj3d, bias):
    bsz, seq, _ = proj3d.shape
    n_pairs = N_HEADS // 2
    col0 = 3 * D_MODEL // LANES
    per = D_MODEL // LANES
    blk = (1, seq, LANES)
    stat = pltpu.VMEM((seq, LANES), F32)
    return pl.pallas_call(
        _attn_kernel,
        grid=(n_pairs, bsz),
        in_specs=[
            pl.BlockSpec(blk, lambda hp, b: (b, 0, col0 + hp)),
            pl.BlockSpec(blk, lambda hp, b: (b, 0, col0 + per + hp)),
            pl.BlockSpec(blk, lambda hp, b: (b, 0, col0 + 2 * per + hp)),
            pl.BlockSpec((1, len(DILATIONS), 2 * BLOCK, 2 * BLOCK),
                         lambda hp, b: (hp, 0, 0, 0)),
        ],
        out_specs=pl.BlockSpec(blk, lambda hp, b: (b, 0, hp)),
        out_shape=jax.ShapeDtypeStruct((bsz, seq, D_MODEL), BF16),
        scratch_shapes=[stat] * 9,
        compiler_params=pltpu.CompilerParams(
            dimension_semantics=("arbitrary", "arbitrary"),
            vmem_limit_bytes=VMEM_LIMIT),
    )(proj3d, proj3d, proj3d, bias)


def _mix_kernel(u_ref, gv_ref, gate_ref, att_ref, bgate_ref, sa_ref, sb_ref, x_ref,
                ws_ref, bs_ref, gvw_ref, wpa_ref, wpb_ref, wout_ref, gfin_ref,
                out_ref, ya_ref, *, final_norm):
    tm = x_ref.shape[0]
    gv = gv_ref[...].astype(F32)
    ms = jnp.mean(gv * gv, axis=-1, keepdims=True)
    vn = (gv * lax.rsqrt(ms + EPS) * gvw_ref[...]).astype(BF16)

    row = lax.broadcasted_iota(jnp.int32, (CHUNK, CHUNK), 0)
    col = lax.broadcasted_iota(jnp.int32, (CHUNK, CHUNK), 1)
    causal = col <= row
    for g in range(A_GROUPS):
        w = jnp.where(causal, ws_ref[g], jnp.zeros((), BF16))
        cs = slice(g * CHUNK, (g + 1) * CHUNK)
        for c in range(tm // CHUNK):
            rs = slice(c * CHUNK, (c + 1) * CHUNK)
            mixed = jnp.dot(w, vn[rs, cs], preferred_element_type=F32) + bs_ref[:, cs]
            ya = u_ref[rs, cs].astype(F32) * mixed * gate_ref[rs, cs].astype(F32)
            ya_ref[rs, cs] = ya.astype(BF16)

    za = jnp.dot(ya_ref[...], wpa_ref[...], preferred_element_type=F32)
    yb = (att_ref[...].astype(F32) * bgate_ref[...].astype(F32)).astype(BF16)
    zb = jnp.dot(yb, wpb_ref[...], preferred_element_type=F32)
    merged = (sa_ref[...].astype(F32) * za + sb_ref[...].astype(F32) * zb).astype(BF16)
    xn = x_ref[...] + jnp.dot(merged, wout_ref[...], preferred_element_type=F32)
    if final_norm:
        ms2 = jnp.mean(xn * xn, axis=-1, keepdims=True)
        xn = xn * lax.rsqrt(ms2 + EPS) * gfin_ref[...]
    out_ref[...] = xn


def _mix(proj, att2d, x2d, ws_bf16, bs_full, gvw, wpa, wpb, wout, gfin, final_norm):
    m = x2d.shape[0]
    tile = (TM_MIX, D_MODEL)

    def col_block(jc):
        return pl.BlockSpec(tile, lambda i: (i, jc))

    def whole(shape):
        return pl.BlockSpec(shape, lambda i: (0,) * len(shape))

    return pl.pallas_call(
        functools.partial(_mix_kernel, final_norm=final_norm),
        grid=(m // TM_MIX,),
        in_specs=[
            col_block(0), col_block(1), col_block(2),
            col_block(0),
            col_block(6), col_block(7), col_block(8),
            col_block(0),
            whole((A_GROUPS, CHUNK, CHUNK)),
            whole((CHUNK, D_MODEL)),
            whole((1, D_MODEL)),
            whole((D_MODEL, D_MODEL)), whole((D_MODEL, D_MODEL)), whole((D_MODEL, D_MODEL)),
            whole((1, D_MODEL)),
        ],
        out_specs=col_block(0),
        out_shape=jax.ShapeDtypeStruct((m, D_MODEL), F32),
        scratch_shapes=[pltpu.VMEM(tile, BF16)],
        compiler_params=pltpu.CompilerParams(
            dimension_semantics=("arbitrary",),
            vmem_limit_bytes=VMEM_LIMIT),
    )(proj, proj, proj, att2d, proj, proj, proj, x2d,
      ws_bf16, bs_full, gvw, wpa, wpb, wout, gfin)


def kernel(x, g_norm, w_in, w_s, b_s, g_v, w_proj_a, w_proj_b, w_out, g_final):
    bsz, seq, dm = x.shape
    assert dm == D_MODEL and seq % (DILATIONS[-1] * BLOCK) == 0
    assert (bsz * seq) % TM_IN == 0 and (bsz * seq) % TM_MIX == 0
    m = bsz * seq
    x2d = x.reshape(m, dm)
    bias = jnp.asarray(_attn_bias_tables())
    w_in_b = w_in.astype(BF16)
    w_s_b = w_s.astype(BF16)
    wpa_b = w_proj_a.astype(BF16)
    wpb_b = w_proj_b.astype(BF16)
    wout_b = w_out.astype(BF16)
    bs_full = jnp.repeat(jnp.swapaxes(b_s, 1, 2), CHUNK, axis=2)
    gfin = g_final.reshape(1, dm)
    for layer in range(DEPTH):
        proj = _inproj(x2d, g_norm[layer].reshape(1, dm), w_in_b[layer])
        att = _attn(proj.reshape(bsz, seq, IN_COLS), bias)
        x2d = _mix(proj, att.reshape(m, dm), x2d, w_s_b[layer], bs_full[layer],
                   g_v[layer].reshape(1, dm), wpa_b[layer], wpb_b[layer], wout_b[layer],
                   gfin, final_norm=(layer == DEPTH - 1))
    return x2d.reshape(bsz, seq, dm)
```

```python
import functools
import math

import numpy as np
import jax
import jax.numpy as jnp
from jax import lax
from jax.experimental import pallas as pl
from jax.experimental.pallas import tpu as pltpu

D_MODEL = 1024
DEPTH = 4
CHUNK = 128
A_GROUPS = 8
N_HEADS = 16
HEAD_DIM = 64
DILATIONS = (1, 4, 16)
BLOCK = 128
EPS = 1e-6
NEG_INF = -1e30
N_COL_BLOCKS = 9
IN_COLS = N_COL_BLOCKS * D_MODEL

LANES = 128
VMEM_LIMIT = 56 * 1024 * 1024

TM_IN = 1024
TM_MIX = 512
ATTN_INTERLEAVE = 4

F32 = jnp.float32
BF16 = jnp.bfloat16


_K1 = 2.0 * math.sqrt(2.0 / math.pi)
_K3 = _K1 * 0.044715
_GELU = (1.0, 0.0, 0.0, 1.0, _K1, _K3)
_SILU = (1.0, 0.0, 0.0, 1.0, 1.0, 0.0)
_SIGM = (0.0, 1.0, 0.0, 1.0, 1.0, 0.0)
_IDEN = (1.0, 0.0, 1.0, 0.0, 0.0, 0.0)
_QSCL = (HEAD_DIM ** -0.5, 0.0, 1.0, 0.0, 0.0, 0.0)
_COEF = np.array([_GELU, _GELU, _SILU, _QSCL, _IDEN, _IDEN, _SILU, _SIGM, _SIGM],
                 dtype=np.float32)


def _inproj_kernel(coef_ref, x_ref, g_ref, w_ref, o_ref, h_ref):
    j = pl.program_id(1)

    @pl.when(j == 0)
    def _():
        x = x_ref[...]
        ms = jnp.mean(x * x, axis=-1, keepdims=True)
        h_ref[...] = (x * lax.rsqrt(ms + EPS) * g_ref[...]).astype(BF16)

    y = jnp.dot(h_ref[...], w_ref[...], preferred_element_type=F32)
    a = coef_ref[j, 0]
    b = coef_ref[j, 1]
    c = coef_ref[j, 2]
    d = coef_ref[j, 3]
    k1 = coef_ref[j, 4]
    k3 = coef_ref[j, 5]
    z = y * (k1 + k3 * (y * y))
    sig = 1.0 / (1.0 + jnp.exp(-z))
    o_ref[...] = ((a * y + b) * (c + d * sig)).astype(BF16)


def _inproj(x2d, g_all, w_all, layer):
    m = x2d.shape[0]
    grid = (m // TM_IN, N_COL_BLOCKS)
    return pl.pallas_call(
        _inproj_kernel,
        name="inproj",
        grid_spec=pltpu.PrefetchScalarGridSpec(
            num_scalar_prefetch=1,
            grid=grid,
            in_specs=[
                pl.BlockSpec((TM_IN, D_MODEL), lambda i, j, coef: (i, 0)),
                pl.BlockSpec((None, 1, D_MODEL), lambda i, j, coef: (layer, 0, 0)),
                pl.BlockSpec((None, D_MODEL, D_MODEL), lambda i, j, coef: (layer, 0, j)),
            ],
            out_specs=pl.BlockSpec((TM_IN, D_MODEL), lambda i, j, coef: (i, j)),
            scratch_shapes=[pltpu.VMEM((TM_IN, D_MODEL), BF16)],
        ),
        out_shape=jax.ShapeDtypeStruct((m, IN_COLS), BF16),
        compiler_params=pltpu.CompilerParams(
            dimension_semantics=("arbitrary", "arbitrary"),
            vmem_limit_bytes=VMEM_LIMIT),
    )(jnp.asarray(_COEF), x2d, g_all, w_all)


def _attn_bias_tables():
    slopes = 2.0 ** (-8.0 * np.arange(1, N_HEADS + 1, dtype=np.float64) / N_HEADS)
    qpos = np.arange(BLOCK)[:, None] + BLOCK
    kpos = np.arange(2 * BLOCK)[None, :]
    dist = qpos - kpos
    valid = (dist >= 0) & (dist <= BLOCK)
    out = np.zeros((N_HEADS // 2, len(DILATIONS), 2 * BLOCK, 2 * BLOCK), np.float32)
    for hp in range(N_HEADS // 2):
        for p, dil in enumerate(DILATIONS):
            for h in range(2):
                pen = -slopes[2 * hp + h] * (dist * dil)
                out[hp, p, h * BLOCK:(h + 1) * BLOCK] = np.where(valid, pen, NEG_INF)
    return out


def _attn_kernel(q_ref, k_ref, v_ref, bias_ref, o_ref,
                 qf, kf, vf, acc2, l2, m2, acc3, l3, m3):
    seq = q_ref.shape[1]
    qf[...] = q_ref[0].astype(F32)
    kf[...] = k_ref[0].astype(F32)
    vf[...] = v_ref[0].astype(F32)

    lane = lax.broadcasted_iota(jnp.int32, (BLOCK, LANES), 1)
    head0 = lane < HEAD_DIM
    ones_cols = jnp.ones((2 * BLOCK, LANES), BF16)

    def rows(ref, start, size, stride):
        if stride == 1:
            return ref[pl.ds(start, size), :]
        return ref[pl.ds(start, size, stride=stride), :]

    def block(p, qs, ks, nk):
        stride = DILATIONS[p]
        q = rows(qf, qs, BLOCK, stride)
        k = rows(kf, ks, nk, stride).astype(BF16)
        v = rows(vf, ks, nk, stride).astype(BF16)
        q2 = jnp.concatenate([jnp.where(head0, q, 0.0), jnp.where(head0, 0.0, q)],
                             axis=0).astype(BF16)
        s = lax.dot_general(q2, k, (((1,), (1,)), ((), ())),
                            preferred_element_type=F32)
        if nk == 2 * BLOCK:
            s = s + bias_ref[0, p]
        else:
            s = s + bias_ref[0, p, :, BLOCK:]
        m = jnp.max(s, axis=-1, keepdims=True)
        pr = jnp.exp(s - m).astype(BF16)
        vext = jnp.concatenate([v, ones_cols[:nk]], axis=1)
        r = jnp.dot(pr, vext, preferred_element_type=F32)
        acc = jnp.where(head0, r[:BLOCK, :LANES], r[BLOCK:, :LANES])
        l = jnp.where(head0, r[:BLOCK, LANES:], r[BLOCK:, LANES:])
        mb = jnp.where(head0, jnp.broadcast_to(m[:BLOCK], (BLOCK, LANES)),
                       jnp.broadcast_to(m[BLOCK:], (BLOCK, LANES)))
        return acc, l, mb

    def scatter(p, refs, qs, vals):
        stride = DILATIONS[p]
        for ref, val in zip(refs, vals):
            ref[pl.ds(qs, BLOCK, stride=stride), :] = val

    u = ATTN_INTERLEAVE

    def p3_body(g, carry):
        starts = [g * u + t for t in range(u)]
        vals = [block(2, r, r, BLOCK) for r in starts]
        for r, val in zip(starts, vals):
            scatter(2, (acc3, l3, m3), r, val)
        return carry
    lax.fori_loop(0, DILATIONS[2] // u, p3_body, 0)

    d2 = DILATIONS[1]
    nb2 = seq // d2 // BLOCK
    vals = [block(1, r, r, BLOCK) for r in range(d2)]
    for r, val in enumerate(vals):
        scatter(1, (acc2, l2, m2), r, val)

    def p2_body(n, carry):
        starts = [r + d2 * BLOCK * n for r in range(d2)]
        vals = [block(1, qs, qs - d2 * BLOCK, 2 * BLOCK) for qs in starts]
        for qs, val in zip(starts, vals):
            scatter(1, (acc2, l2, m2), qs, val)
        return carry
    lax.fori_loop(1, nb2, p2_body, 0)

    def finish(qs, vals):
        acc1, l1, m1 = vals
        a2 = acc2[pl.ds(qs, BLOCK), :]
        a3 = acc3[pl.ds(qs, BLOCK), :]
        ll2 = l2[pl.ds(qs, BLOCK), :]
        ll3 = l3[pl.ds(qs, BLOCK), :]
        mm2 = m2[pl.ds(qs, BLOCK), :]
        mm3 = m3[pl.ds(qs, BLOCK), :]
        mx = jnp.maximum(jnp.maximum(m1, mm2), mm3)
        e1 = jnp.exp(m1 - mx)
        e2 = jnp.exp(mm2 - mx)
        e3 = jnp.exp(mm3 - mx)
        num = e1 * acc1 + e2 * a2 + e3 * a3
        den = e1 * l1 + e2 * ll2 + e3 * ll3
        o_ref[0, pl.ds(qs, BLOCK), :] = (num / den).astype(o_ref.dtype)

    vals = [block(0, 0, 0, BLOCK)]
    vals += [block(0, t * BLOCK, (t - 1) * BLOCK, 2 * BLOCK) for t in range(1, u)]
    for t, val in enumerate(vals):
        finish(t * BLOCK, val)

    def p1_body(g, carry):
        starts = [pl.multiple_of((g * u + t) * BLOCK, BLOCK) for t in range(u)]
        vals = [block(0, qs, pl.multiple_of(qs - BLOCK, BLOCK), 2 * BLOCK) for qs in starts]
        for qs, val in zip(starts, vals):
            finish(qs, val)
        return carry
    lax.fori_loop(1, seq // BLOCK // u, p1_body, 0)


def _attn(proj3d, bias):
    bsz, seq, _ = proj3d.shape
    n_pairs = N_HEADS // 2
    col0 = 3 * D_MODEL // LANES
    per = D_MODEL // LANES
    blk = (1, seq, LANES)
    stat = pltpu.VMEM((seq, LANES), F32)
    return pl.pallas_call(
        _attn_kernel,
        name="attn",
        grid=(n_pairs, bsz),
        in_specs=[
            pl.BlockSpec(blk, lambda hp, b: (b, 0, col0 + hp)),
            pl.BlockSpec(blk, lambda hp, b: (b, 0, col0 + per + hp)),
            pl.BlockSpec(blk, lambda hp, b: (b, 0, col0 + 2 * per + hp)),
            pl.BlockSpec((1, len(DILATIONS), 2 * BLOCK, 2 * BLOCK),
                         lambda hp, b: (hp, 0, 0, 0)),
        ],
        out_specs=pl.BlockSpec(blk, lambda hp, b: (b, 0, hp)),
        out_shape=jax.ShapeDtypeStruct((bsz, seq, D_MODEL), BF16),
        scratch_shapes=[stat] * 9,
        compiler_params=pltpu.CompilerParams(
            dimension_semantics=("arbitrary", "arbitrary"),
            vmem_limit_bytes=VMEM_LIMIT),
    )(proj3d, proj3d, proj3d, bias)


def _mix_kernel(u_ref, gv_ref, gate_ref, att_ref, bgate_ref, sa_ref, sb_ref, x_ref,
                ws_ref, bs_ref, gvw_ref, wpa_ref, wpb_ref, wout_ref, gfin_ref,
                out_ref, ya_ref, *, final_norm):
    tm = x_ref.shape[0]
    gv = gv_ref[...].astype(F32)
    ms = jnp.mean(gv * gv, axis=-1, keepdims=True)
    vn = (gv * lax.rsqrt(ms + EPS) * gvw_ref[...]).astype(BF16)

    row = lax.broadcasted_iota(jnp.int32, (CHUNK, CHUNK), 0)
    col = lax.broadcasted_iota(jnp.int32, (CHUNK, CHUNK), 1)
    causal = col <= row
    for g in range(A_GROUPS):
        w = jnp.where(causal, ws_ref[g], jnp.zeros((), BF16))
        cs = slice(g * CHUNK, (g + 1) * CHUNK)
        for c in range(tm // CHUNK):
            rs = slice(c * CHUNK, (c + 1) * CHUNK)
            mixed = jnp.dot(w, vn[rs, cs], preferred_element_type=F32) + bs_ref[:, cs]
            ya = u_ref[rs, cs].astype(F32) * mixed * gate_ref[rs, cs].astype(F32)
            ya_ref[rs, cs] = ya.astype(BF16)

    za = jnp.dot(ya_ref[...], wpa_ref[...], preferred_element_type=F32)
    yb = (att_ref[...].astype(F32) * bgate_ref[...].astype(F32)).astype(BF16)
    zb = jnp.dot(yb, wpb_ref[...], preferred_element_type=F32)
    merged = (sa_ref[...].astype(F32) * za + sb_ref[...].astype(F32) * zb).astype(BF16)
    xn = x_ref[...] + jnp.dot(merged, wout_ref[...], preferred_element_type=F32)
    if final_norm:
        ms2 = jnp.mean(xn * xn, axis=-1, keepdims=True)
        xn = xn * lax.rsqrt(ms2 + EPS) * gfin_ref[...]
    out_ref[...] = xn


def _mix(proj, att2d, x2d, ws_bf16, bs_full, gvw, wpa, wpb, wout, gfin, layer, final_norm):
    m = x2d.shape[0]
    tile = (TM_MIX, D_MODEL)

    def col_block(jc):
        return pl.BlockSpec(tile, lambda i: (i, jc))

    def whole(shape):
        return pl.BlockSpec((None,) + shape, lambda i: (layer,) + (0,) * len(shape))

    return pl.pallas_call(
        functools.partial(_mix_kernel, final_norm=final_norm),
        name="mix",
        grid=(m // TM_MIX,),
        in_specs=[
            col_block(0), col_block(1), col_block(2),
            col_block(0),
            col_block(6), col_block(7), col_block(8),
            col_block(0),
            whole((A_GROUPS, CHUNK, CHUNK)),
            whole((CHUNK, D_MODEL)),
            whole((1, D_MODEL)),
            whole((D_MODEL, D_MODEL)), whole((D_MODEL, D_MODEL)), whole((D_MODEL, D_MODEL)),
            pl.BlockSpec((1, D_MODEL), lambda i: (0, 0)),
        ],
        out_specs=col_block(0),
        out_shape=jax.ShapeDtypeStruct((m, D_MODEL), F32),
        scratch_shapes=[pltpu.VMEM(tile, BF16)],
        compiler_params=pltpu.CompilerParams(
            dimension_semantics=("arbitrary",),
            vmem_limit_bytes=VMEM_LIMIT),
    )(proj, proj, proj, att2d, proj, proj, proj, x2d,
      ws_bf16, bs_full, gvw, wpa, wpb, wout, gfin)


def kernel(x, g_norm, w_in, w_s, b_s, g_v, w_proj_a, w_proj_b, w_out, g_final):
    bsz, seq, dm = x.shape
    assert dm == D_MODEL and seq % (DILATIONS[-1] * BLOCK) == 0
    assert (bsz * seq) % TM_IN == 0 and (bsz * seq) % TM_MIX == 0
    m = bsz * seq
    x2d = x.reshape(m, dm)
    bias = jnp.asarray(_attn_bias_tables())
    w_in_b = w_in.astype(BF16)
    w_s_b = w_s.astype(BF16)
    wpa_b = w_proj_a.astype(BF16)
    wpb_b = w_proj_b.astype(BF16)
    wout_b = w_out.astype(BF16)
    bs_full = jnp.repeat(jnp.swapaxes(b_s, 1, 2), CHUNK, axis=2)
    gfin = g_final.reshape(1, dm)
    g_norm3 = g_norm.reshape(DEPTH, 1, dm)
    g_v3 = g_v.reshape(DEPTH, 1, dm)
    for layer in range(DEPTH):
        proj = _inproj(x2d, g_norm3, w_in_b, layer)
        att = _attn(proj.reshape(bsz, seq, IN_COLS), bias)
        x2d = _mix(proj, att.reshape(m, dm), x2d, w_s_b, bs_full, g_v3, wpa_b, wpb_b, wout_b,
                   gfin, layer, final_norm=(layer == DEPTH - 1))
    return x2d.reshape(bsz, seq, dm)
```

```python
import functools
import math

import numpy as np
import jax
import jax.numpy as jnp
from jax import lax
from jax.experimental import pallas as pl
from jax.experimental.pallas import tpu as pltpu

D_MODEL = 1024
DEPTH = 4
CHUNK = 128
A_GROUPS = 8
N_HEADS = 16
HEAD_DIM = 64
DILATIONS = (1, 4, 16)
BLOCK = 128
EPS = 1e-6
NEG_INF = -1e30
N_COL_BLOCKS = 9
IN_COLS = N_COL_BLOCKS * D_MODEL

LANES = 128
VMEM_LIMIT = 56 * 1024 * 1024

TM_IN = 1024
TM_MIX = 512

F32 = jnp.float32
BF16 = jnp.bfloat16


_LOG2E = math.log2(math.e)
_GELU_K1 = -2.0 * math.sqrt(2.0 / math.pi) * _LOG2E
_GELU_K3 = _GELU_K1 * 0.044715
_Q_SCALE = HEAD_DIM ** -0.5 * _LOG2E


def _gelu(y):
    return y / (1.0 + jnp.exp2(y * (_GELU_K1 + _GELU_K3 * (y * y))))


def _silu(y):
    return y / (1.0 + jnp.exp2(y * -_LOG2E))


def _sigmoid(y):
    return 1.0 / (1.0 + jnp.exp2(y * -_LOG2E))


def _inproj_kernel(x_ref, g_ref, w_ref, o_ref, h_ref):
    j = pl.program_id(1)

    @pl.when(j == 0)
    def _():
        x = x_ref[...]
        ms = jnp.mean(x * x, axis=-1, keepdims=True)
        h_ref[...] = (x * lax.rsqrt(ms + EPS) * g_ref[...]).astype(BF16)

    def project(act):
        y = jnp.dot(h_ref[...], w_ref[...], preferred_element_type=F32)
        o_ref[...] = act(y).astype(BF16)

    @pl.when(j < 2)
    def _():
        project(_gelu)

    @pl.when((j == 2) | (j == 6))
    def _():
        project(_silu)

    @pl.when((j >= 3) & (j <= 5))
    def _():
        scale = jnp.where(j == 3, _Q_SCALE, 1.0).astype(F32)
        project(lambda y: y * scale)

    @pl.when(j >= 7)
    def _():
        project(_sigmoid)


def _inproj(x2d, g_all, w_all, layer):
    m = x2d.shape[0]
    grid = (m // TM_IN, N_COL_BLOCKS)
    return pl.pallas_call(
        _inproj_kernel,
        name="inproj",
        grid=grid,
        in_specs=[
            pl.BlockSpec((TM_IN, D_MODEL), lambda i, j: (i, 0)),
            pl.BlockSpec((None, 1, D_MODEL), lambda i, j: (layer, 0, 0)),
            pl.BlockSpec((None, D_MODEL, D_MODEL), lambda i, j: (layer, 0, j)),
        ],
        out_specs=pl.BlockSpec((TM_IN, D_MODEL), lambda i, j: (i, j)),
        scratch_shapes=[pltpu.VMEM((TM_IN, D_MODEL), BF16)],
        out_shape=jax.ShapeDtypeStruct((m, IN_COLS), BF16),
        compiler_params=pltpu.CompilerParams(
            dimension_semantics=("arbitrary", "arbitrary"),
            vmem_limit_bytes=VMEM_LIMIT),
    )(x2d, g_all, w_all)


def _attn_bias_tables():
    slopes = 2.0 ** (-8.0 * np.arange(1, N_HEADS + 1, dtype=np.float64) / N_HEADS)
    qpos = np.arange(BLOCK)[:, None] + BLOCK
    kpos = np.arange(2 * BLOCK)[None, :]
    dist = qpos - kpos
    valid = (dist >= 0) & (dist <= BLOCK)
    out = np.zeros((N_HEADS // 2, len(DILATIONS), 2 * BLOCK, 2 * BLOCK), np.float32)
    for hp in range(N_HEADS // 2):
        for p, dil in enumerate(DILATIONS):
            for h in range(2):
                pen = -slopes[2 * hp + h] * (dist * dil) * _LOG2E
                out[hp, p, h * BLOCK:(h + 1) * BLOCK] = np.where(valid, pen, NEG_INF)
    return out


def _attn_kernel(q_ref, k_ref, v_ref, bias_ref, o_ref,
                 qf, kf, vf, acc2, l2, m2, acc3, l3, m3):
    seq = q_ref.shape[1]
    qf[...] = q_ref[0].astype(F32)
    kf[...] = k_ref[0].astype(F32)
    vf[...] = v_ref[0].astype(F32)

    lane = lax.broadcasted_iota(jnp.int32, (BLOCK, LANES), 1)
    head0 = lane < HEAD_DIM
    ones_cols = jnp.ones((2 * BLOCK, LANES), BF16)

    def rows(ref, start, size, stride):
        if stride == 1:
            return ref[pl.ds(start, size), :]
        return ref[pl.ds(start, size, stride=stride), :]

    def block(p, qs, ks, nk):
        stride = DILATIONS[p]
        q = rows(qf, qs, BLOCK, stride)
        k = rows(kf, ks, nk, stride).astype(BF16)
        v = rows(vf, ks, nk, stride).astype(BF16)
        q2 = jnp.concatenate([jnp.where(head0, q, 0.0), jnp.where(head0, 0.0, q)],
                             axis=0).astype(BF16)
        s = lax.dot_general(q2, k, (((1,), (1,)), ((), ())),
                            preferred_element_type=F32)
        if nk == 2 * BLOCK:
            s = s + bias_ref[0, p]
        else:
            s = s + bias_ref[0, p, :, BLOCK:]
        m = jnp.max(s, axis=-1, keepdims=True)
        pr = jnp.exp2(s - m).astype(BF16)
        vext = jnp.concatenate([v, ones_cols[:nk]], axis=1)
        r = jnp.dot(pr, vext, preferred_element_type=F32)
        acc = jnp.where(head0, r[:BLOCK, :LANES], r[BLOCK:, :LANES])
        l = jnp.where(head0, r[:BLOCK, LANES:], r[BLOCK:, LANES:])
        mb = jnp.where(head0, jnp.broadcast_to(m[:BLOCK], (BLOCK, LANES)),
                       jnp.broadcast_to(m[BLOCK:], (BLOCK, LANES)))
        return acc, l, mb

    def scatter(p, refs, qs, vals):
        stride = DILATIONS[p]
        for ref, val in zip(refs, vals):
            ref[pl.ds(qs, BLOCK, stride=stride), :] = val

    def pattern_blocks(p):
        d = DILATIONS[p]
        out = []
        for n in range(seq // d // BLOCK):
            for r in range(d):
                qs = r + d * BLOCK * n
                if n == 0:
                    out.append((qs, block(p, qs, qs, BLOCK)))
                else:
                    out.append((qs, block(p, qs, qs - d * BLOCK, 2 * BLOCK)))
        return out

    for p, refs in ((2, (acc3, l3, m3)), (1, (acc2, l2, m2))):
        for qs, val in pattern_blocks(p):
            scatter(p, refs, qs, val)

    def finish(qs, vals):
        acc1, l1, m1 = vals
        a2 = acc2[pl.ds(qs, BLOCK), :]
        a3 = acc3[pl.ds(qs, BLOCK), :]
        ll2 = l2[pl.ds(qs, BLOCK), :]
        ll3 = l3[pl.ds(qs, BLOCK), :]
        mm2 = m2[pl.ds(qs, BLOCK), :]
        mm3 = m3[pl.ds(qs, BLOCK), :]
        mx = jnp.maximum(jnp.maximum(m1, mm2), mm3)
        e1 = jnp.exp2(m1 - mx)
        e2 = jnp.exp2(mm2 - mx)
        e3 = jnp.exp2(mm3 - mx)
        num = e1 * acc1 + e2 * a2 + e3 * a3
        den = e1 * l1 + e2 * ll2 + e3 * ll3
        o_ref[0, pl.ds(qs, BLOCK), :] = (num / den).astype(o_ref.dtype)

    for qs, val in pattern_blocks(0):
        finish(qs, val)


def _attn(proj3d, bias):
    bsz, seq, _ = proj3d.shape
    n_pairs = N_HEADS // 2
    col0 = 3 * D_MODEL // LANES
    per = D_MODEL // LANES
    blk = (1, seq, LANES)
    stat = pltpu.VMEM((seq, LANES), F32)
    return pl.pallas_call(
        _attn_kernel,
        name="attn",
        grid=(n_pairs, bsz),
        in_specs=[
            pl.BlockSpec(blk, lambda hp, b: (b, 0, col0 + hp)),
            pl.BlockSpec(blk, lambda hp, b: (b, 0, col0 + per + hp)),
            pl.BlockSpec(blk, lambda hp, b: (b, 0, col0 + 2 * per + hp)),
            pl.BlockSpec((1, len(DILATIONS), 2 * BLOCK, 2 * BLOCK),
                         lambda hp, b: (hp, 0, 0, 0)),
        ],
        out_specs=pl.BlockSpec(blk, lambda hp, b: (b, 0, hp)),
        out_shape=jax.ShapeDtypeStruct((bsz, seq, D_MODEL), BF16),
        scratch_shapes=[stat] * 9,
        compiler_params=pltpu.CompilerParams(
            dimension_semantics=("arbitrary", "arbitrary"),
            vmem_limit_bytes=VMEM_LIMIT),
    )(proj3d, proj3d, proj3d, bias)


def _mix_kernel(u_ref, gv_ref, gate_ref, att_ref, bgate_ref, sa_ref, sb_ref, x_ref,
                ws_ref, bs_ref, gvw_ref, wpa_ref, wpb_ref, wout_ref, gfin_ref,
                out_ref, ya_ref, *, final_norm):
    tm = x_ref.shape[0]
    gv = gv_ref[...].astype(F32)
    ms = jnp.mean(gv * gv, axis=-1, keepdims=True)
    vn = (gv * lax.rsqrt(ms + EPS) * gvw_ref[...]).astype(BF16)

    row = lax.broadcasted_iota(jnp.int32, (CHUNK, CHUNK), 0)
    col = lax.broadcasted_iota(jnp.int32, (CHUNK, CHUNK), 1)
    causal = col <= row
    for g in range(A_GROUPS):
        w = jnp.where(causal, ws_ref[g], jnp.zeros((), BF16))
        cs = slice(g * CHUNK, (g + 1) * CHUNK)
        for c in range(tm // CHUNK):
            rs = slice(c * CHUNK, (c + 1) * CHUNK)
            mixed = jnp.dot(w, vn[rs, cs], preferred_element_type=F32) + bs_ref[:, cs]
            ya = u_ref[rs, cs].astype(F32) * mixed * gate_ref[rs, cs].astype(F32)
            ya_ref[rs, cs] = ya.astype(BF16)

    za = jnp.dot(ya_ref[...], wpa_ref[...], preferred_element_type=F32)
    yb = (att_ref[...].astype(F32) * bgate_ref[...].astype(F32)).astype(BF16)
    zb = jnp.dot(yb, wpb_ref[...], preferred_element_type=F32)
    merged = (sa_ref[...].astype(F32) * za + sb_ref[...].astype(F32) * zb).astype(BF16)
    xn = x_ref[...] + jnp.dot(merged, wout_ref[...], preferred_element_type=F32)
    if final_norm:
        ms2 = jnp.mean(xn * xn, axis=-1, keepdims=True)
        xn = xn * lax.rsqrt(ms2 + EPS) * gfin_ref[...]
    out_ref[...] = xn


def _mix(proj, att2d, x2d, ws_bf16, bs_full, gvw, wpa, wpb, wout, gfin, layer, final_norm):
    m = x2d.shape[0]
    tile = (TM_MIX, D_MODEL)

    def col_block(jc):
        return pl.BlockSpec(tile, lambda i: (i, jc))

    def whole(shape):
        return pl.BlockSpec((None,) + shape, lambda i: (layer,) + (0,) * len(shape))

    return pl.pallas_call(
        functools.partial(_mix_kernel, final_norm=final_norm),
        name="mix",
        grid=(m // TM_MIX,),
        in_specs=[
            col_block(0), col_block(1), col_block(2),
            col_block(0),
            col_block(6), col_block(7), col_block(8),
            col_block(0),
            whole((A_GROUPS, CHUNK, CHUNK)),
            whole((CHUNK, D_MODEL)),
            whole((1, D_MODEL)),
            whole((D_MODEL, D_MODEL)), whole((D_MODEL, D_MODEL)), whole((D_MODEL, D_MODEL)),
            pl.BlockSpec((1, D_MODEL), lambda i: (0, 0)),
        ],
        out_specs=col_block(0),
        out_shape=jax.ShapeDtypeStruct((m, D_MODEL), F32),
        scratch_shapes=[pltpu.VMEM(tile, BF16)],
        compiler_params=pltpu.CompilerParams(
            dimension_semantics=("arbitrary",),
            vmem_limit_bytes=VMEM_LIMIT),
    )(proj, proj, proj, att2d, proj, proj, proj, x2d,
      ws_bf16, bs_full, gvw, wpa, wpb, wout, gfin)


def kernel(x, g_norm, w_in, w_s, b_s, g_v, w_proj_a, w_proj_b, w_out, g_final):
    bsz, seq, dm = x.shape
    assert dm == D_MODEL and seq % (DILATIONS[-1] * BLOCK) == 0
    assert (bsz * seq) % TM_IN == 0 and (bsz * seq) % TM_MIX == 0
    m = bsz * seq
    x2d = x.reshape(m, dm)
    bias = jnp.asarray(_attn_bias_tables())
    w_in_b = w_in.astype(BF16)
    w_s_b = w_s.astype(BF16)
    wpa_b = w_proj_a.astype(BF16)
    wpb_b = w_proj_b.astype(BF16)
    wout_b = w_out.astype(BF16)
    bs_full = jnp.repeat(jnp.swapaxes(b_s, 1, 2), CHUNK, axis=2)
    gfin = g_final.reshape(1, dm)
    g_norm3 = g_norm.reshape(DEPTH, 1, dm)
    g_v3 = g_v.reshape(DEPTH, 1, dm)
    for layer in range(DEPTH):
        proj = _inproj(x2d, g_norm3, w_in_b, layer)
        att = _attn(proj.reshape(bsz, seq, IN_COLS), bias)
        x2d = _mix(proj, att.reshape(m, dm), x2d, w_s_b, bs_full, g_v3, wpa_b, wpb_b, wout_b,
                   gfin, layer, final_norm=(layer == DEPTH - 1))
    return x2d.reshape(bsz, seq, dm)
```

```python
import functools
import math

import numpy as np
import jax
import jax.numpy as jnp
from jax import lax
from jax.experimental import pallas as pl
from jax.experimental.pallas import tpu as pltpu

D_MODEL = 1024
DEPTH = 4
CHUNK = 128
A_GROUPS = 8
N_HEADS = 16
HEAD_DIM = 64
DILATIONS = (1, 4, 16)
BLOCK = 128
EPS = 1e-6
NEG_INF = -1e30
N_COL_BLOCKS = 9
IN_COLS = N_COL_BLOCKS * D_MODEL

LANES = 128
MXU_WIDTH = 256
VMEM_LIMIT = 56 * 1024 * 1024

TM_IN = 1024
TM_MIX = 512

F32 = jnp.float32
BF16 = jnp.bfloat16


_LOG2E = math.log2(math.e)
_GELU_C1 = math.sqrt(2.0 / math.pi)
_GELU_C3 = _GELU_C1 * 0.044715
_Q_SCALE = HEAD_DIM ** -0.5 * _LOG2E


def _gelu(y):
    return (0.5 * y) * (1.0 + jnp.tanh(y * (_GELU_C1 + _GELU_C3 * (y * y))))


def _silu(y):
    h = 0.5 * y
    return h * (1.0 + jnp.tanh(h))


def _sigmoid(y):
    return 0.5 + 0.5 * jnp.tanh(0.5 * y)


def _inproj_kernel(x_ref, g_ref, w_ref, o_ref, h_ref):
    j = pl.program_id(1)

    @pl.when(j == 0)
    def _():
        x = x_ref[...]
        ms = jnp.mean(x * x, axis=-1, keepdims=True)
        h_ref[...] = (x * lax.rsqrt(ms + EPS) * g_ref[...]).astype(BF16)

    def project(act):
        for c in range(0, D_MODEL, MXU_WIDTH):
            cs = slice(c, c + MXU_WIDTH)
            y = jnp.dot(h_ref[...], w_ref[:, cs], preferred_element_type=F32)
            o_ref[:, cs] = act(y).astype(BF16)

    @pl.when(j < 2)
    def _():
        project(_gelu)

    @pl.when((j == 2) | (j == 6))
    def _():
        project(_silu)

    @pl.when((j >= 3) & (j <= 5))
    def _():
        scale = jnp.where(j == 3, _Q_SCALE, 1.0).astype(F32)
        project(lambda y: y * scale)

    @pl.when(j >= 7)
    def _():
        project(_sigmoid)


def _inproj(x2d, g_all, w_all, layer):
    m = x2d.shape[0]
    grid = (m // TM_IN, N_COL_BLOCKS)
    return pl.pallas_call(
        _inproj_kernel,
        name="inproj",
        grid=grid,
        in_specs=[
            pl.BlockSpec((TM_IN, D_MODEL), lambda i, j: (i, 0)),
            pl.BlockSpec((None, 1, D_MODEL), lambda i, j: (layer, 0, 0)),
            pl.BlockSpec((None, D_MODEL, D_MODEL), lambda i, j: (layer, 0, j)),
        ],
        out_specs=pl.BlockSpec((TM_IN, D_MODEL), lambda i, j: (i, j)),
        scratch_shapes=[pltpu.VMEM((TM_IN, D_MODEL), BF16)],
        out_shape=jax.ShapeDtypeStruct((m, IN_COLS), BF16),
        compiler_params=pltpu.CompilerParams(
            dimension_semantics=("arbitrary", "arbitrary"),
            vmem_limit_bytes=VMEM_LIMIT),
    )(x2d, g_all, w_all)


def _attn_bias_tables():
    slopes = 2.0 ** (-8.0 * np.arange(1, N_HEADS + 1, dtype=np.float64) / N_HEADS)
    qpos = np.arange(BLOCK)[:, None] + BLOCK
    kpos = np.arange(2 * BLOCK)[None, :]
    dist = qpos - kpos
    valid = (dist >= 0) & (dist <= BLOCK)
    out = np.zeros((N_HEADS // 2, len(DILATIONS), 2 * BLOCK, 2 * BLOCK), np.float32)
    for hp in range(N_HEADS // 2):
        for p, dil in enumerate(DILATIONS):
            for h in range(2):
                pen = -slopes[2 * hp + h] * (dist * dil) * _LOG2E
                out[hp, p, h * BLOCK:(h + 1) * BLOCK] = np.where(valid, pen, NEG_INF)
    return out


def _attn_kernel(q_ref, k_ref, v_ref, bias_ref, o_ref,
                 qf, kf, vf, q0g, q1g, kg, vg, acc2, l2, m2, acc3, l3, m3):
    seq = q_ref.shape[1]

    def head0_mask(n_rows):
        return lax.broadcasted_iota(jnp.int32, (n_rows, LANES), 1) < HEAD_DIM

    q = q_ref[0]
    q0g[0] = jnp.where(head0_mask(seq), q, jnp.zeros_like(q))
    q1g[0] = jnp.where(head0_mask(seq), jnp.zeros_like(q), q)
    qf[...] = q.astype(F32)
    kf[...] = k_ref[0].astype(F32)
    vf[...] = v_ref[0].astype(F32)
    for p in (1, 2):
        d = DILATIONS[p]
        n_sub = seq // d
        h0 = head0_mask(n_sub)
        for r in range(d):
            dst = slice(r * n_sub, (r + 1) * n_sub)
            qr = qf[pl.ds(r, n_sub, stride=d), :]
            q0g[p, dst] = jnp.where(h0, qr, 0.0).astype(BF16)
            q1g[p, dst] = jnp.where(h0, 0.0, qr).astype(BF16)
            kg[p - 1, dst] = kf[pl.ds(r, n_sub, stride=d), :].astype(BF16)
            vg[p - 1, dst] = vf[pl.ds(r, n_sub, stride=d), :].astype(BF16)

    head0 = head0_mask(BLOCK)
    ones_cols = jnp.ones((2 * BLOCK, LANES), BF16)

    def block(p, g0, first):
        nk = BLOCK if first else 2 * BLOCK
        ks = slice(g0 + BLOCK - nk, g0 + BLOCK)
        if p == 0:
            k, v = k_ref[0, ks], v_ref[0, ks]
        else:
            k, v = kg[p - 1, ks], vg[p - 1, ks]
        q2 = jnp.concatenate([q0g[p, g0:g0 + BLOCK], q1g[p, g0:g0 + BLOCK]], axis=0)
        s = lax.dot_general(q2, k, (((1,), (1,)), ((), ())),
                            preferred_element_type=F32)
        if nk == 2 * BLOCK:
            s = s + bias_ref[0, p]
        else:
            s = s + bias_ref[0, p, :, BLOCK:]
        m = jnp.max(s, axis=-1, keepdims=True)
        pr = jnp.exp2(s - m).astype(BF16)
        vext = jnp.concatenate([v, ones_cols[:nk]], axis=1)
        r = jnp.dot(pr, vext, preferred_element_type=F32)
        acc = jnp.where(head0, r[:BLOCK, :LANES], r[BLOCK:, :LANES])
        l = jnp.where(head0, r[:BLOCK, LANES:], r[BLOCK:, LANES:])
        mb = jnp.where(head0, jnp.broadcast_to(m[:BLOCK], (BLOCK, LANES)),
                       jnp.broadcast_to(m[BLOCK:], (BLOCK, LANES)))
        return acc, l, mb

    def scatter(p, refs, qs, vals):
        stride = DILATIONS[p]
        for ref, val in zip(refs, vals):
            ref[pl.ds(qs, BLOCK, stride=stride), :] = val

    def pattern_blocks(p):
        d = DILATIONS[p]
        n_sub = seq // d
        out = []
        for n in range(n_sub // BLOCK):
            for r in range(d):
                out.append((r + d * BLOCK * n, block(p, r * n_sub + n * BLOCK, n == 0)))
        return out

    for p, refs in ((2, (acc3, l3, m3)), (1, (acc2, l2, m2))):
        for qs, val in pattern_blocks(p):
            scatter(p, refs, qs, val)

    def finish(qs, vals):
        acc1, l1, m1 = vals
        a2 = acc2[pl.ds(qs, BLOCK), :]
        a3 = acc3[pl.ds(qs, BLOCK), :]
        ll2 = l2[pl.ds(qs, BLOCK), :]
        ll3 = l3[pl.ds(qs, BLOCK), :]
        mm2 = m2[pl.ds(qs, BLOCK), :]
        mm3 = m3[pl.ds(qs, BLOCK), :]
        mx = jnp.maximum(jnp.maximum(m1, mm2), mm3)
        e1 = jnp.exp2(m1 - mx)
        e2 = jnp.exp2(mm2 - mx)
        e3 = jnp.exp2(mm3 - mx)
        num = e1 * acc1 + e2 * a2 + e3 * a3
        den = e1 * l1 + e2 * ll2 + e3 * ll3
        o_ref[0, pl.ds(qs, BLOCK), :] = (num / den).astype(o_ref.dtype)

    for qs, val in pattern_blocks(0):
        finish(qs, val)


def _attn(proj3d, bias):
    bsz, seq, _ = proj3d.shape
    n_pairs = N_HEADS // 2
    col0 = 3 * D_MODEL // LANES
    per = D_MODEL // LANES
    blk = (1, seq, LANES)
    stat = pltpu.VMEM((seq, LANES), F32)
    return pl.pallas_call(
        _attn_kernel,
        name="attn",
        grid=(n_pairs, bsz),
        in_specs=[
            pl.BlockSpec(blk, lambda hp, b: (b, 0, col0 + hp)),
            pl.BlockSpec(blk, lambda hp, b: (b, 0, col0 + per + hp)),
            pl.BlockSpec(blk, lambda hp, b: (b, 0, col0 + 2 * per + hp)),
            pl.BlockSpec((1, len(DILATIONS), 2 * BLOCK, 2 * BLOCK),
                         lambda hp, b: (hp, 0, 0, 0)),
        ],
        out_specs=pl.BlockSpec(blk, lambda hp, b: (b, 0, hp)),
        out_shape=jax.ShapeDtypeStruct((bsz, seq, D_MODEL), BF16),
        scratch_shapes=[
            stat, stat, stat,
            pltpu.VMEM((len(DILATIONS), seq, LANES), BF16),
            pltpu.VMEM((len(DILATIONS), seq, LANES), BF16),
            pltpu.VMEM((len(DILATIONS) - 1, seq, LANES), BF16),
            pltpu.VMEM((len(DILATIONS) - 1, seq, LANES), BF16),
            stat, stat, stat, stat, stat, stat,
        ],
        compiler_params=pltpu.CompilerParams(
            dimension_semantics=("arbitrary", "arbitrary"),
            vmem_limit_bytes=VMEM_LIMIT),
    )(proj3d, proj3d, proj3d, bias)


def _mix_kernel(u_ref, gv_ref, gate_ref, att_ref, bgate_ref, sa_ref, sb_ref, x_ref,
                ws_ref, bs_ref, gvw_ref, wpa_ref, wpb_ref, wout_ref, gfin_ref,
                out_ref, ya_ref, *, final_norm):
    tm = x_ref.shape[0]
    gv = gv_ref[...].astype(F32)
    ms = jnp.mean(gv * gv, axis=-1, keepdims=True)
    vn = (gv * lax.rsqrt(ms + EPS) * gvw_ref[...]).astype(BF16)

    row = lax.broadcasted_iota(jnp.int32, (CHUNK, CHUNK), 0)
    col = lax.broadcasted_iota(jnp.int32, (CHUNK, CHUNK), 1)
    causal = col <= row
    for g in range(A_GROUPS):
        w = jnp.where(causal, ws_ref[g], jnp.zeros((), BF16))
        cs = slice(g * CHUNK, (g + 1) * CHUNK)
        for c in range(tm // CHUNK):
            rs = slice(c * CHUNK, (c + 1) * CHUNK)
            mixed = jnp.dot(w, vn[rs, cs], preferred_element_type=F32) + bs_ref[:, cs]
            ya = u_ref[rs, cs].astype(F32) * mixed * gate_ref[rs, cs].astype(F32)
            ya_ref[rs, cs] = ya.astype(BF16)

    za = jnp.dot(ya_ref[...], wpa_ref[...], preferred_element_type=F32)
    yb = (att_ref[...].astype(F32) * bgate_ref[...].astype(F32)).astype(BF16)
    zb = jnp.dot(yb, wpb_ref[...], preferred_element_type=F32)
    merged = (sa_ref[...].astype(F32) * za + sb_ref[...].astype(F32) * zb).astype(BF16)
    xn = x_ref[...] + jnp.dot(merged, wout_ref[...], preferred_element_type=F32)
    if final_norm:
        ms2 = jnp.mean(xn * xn, axis=-1, keepdims=True)
        xn = xn * lax.rsqrt(ms2 + EPS) * gfin_ref[...]
    out_ref[...] = xn


def _mix(proj, att2d, x2d, ws_bf16, bs_full, gvw, wpa, wpb, wout, gfin, layer, final_norm):
    m = x2d.shape[0]
    tile = (TM_MIX, D_MODEL)

    def col_block(jc):
        return pl.BlockSpec(tile, lambda i: (i, jc))

    def whole(shape):
        return pl.BlockSpec((None,) + shape, lambda i: (layer,) + (0,) * len(shape))

    return pl.pallas_call(
        functools.partial(_mix_kernel, final_norm=final_norm),
        name="mix",
        grid=(m // TM_MIX,),
        in_specs=[
            col_block(0), col_block(1), col_block(2),
            col_block(0),
            col_block(6), col_block(7), col_block(8),
            col_block(0),
            whole((A_GROUPS, CHUNK, CHUNK)),
            whole((CHUNK, D_MODEL)),
            whole((1, D_MODEL)),
            whole((D_MODEL, D_MODEL)), whole((D_MODEL, D_MODEL)), whole((D_MODEL, D_MODEL)),
            pl.BlockSpec((1, D_MODEL), lambda i: (0, 0)),
        ],
        out_specs=col_block(0),
        out_shape=jax.ShapeDtypeStruct((m, D_MODEL), F32),
        scratch_shapes=[pltpu.VMEM(tile, BF16)],
        compiler_params=pltpu.CompilerParams(
            dimension_semantics=("arbitrary",),
            vmem_limit_bytes=VMEM_LIMIT),
    )(proj, proj, proj, att2d, proj, proj, proj, x2d,
      ws_bf16, bs_full, gvw, wpa, wpb, wout, gfin)


def kernel(x, g_norm, w_in, w_s, b_s, g_v, w_proj_a, w_proj_b, w_out, g_final):
    bsz, seq, dm = x.shape
    assert dm == D_MODEL and seq % (DILATIONS[-1] * BLOCK) == 0
    assert (bsz * seq) % TM_IN == 0 and (bsz * seq) % TM_MIX == 0
    m = bsz * seq
    x2d = x.reshape(m, dm)
    bias = jnp.asarray(_attn_bias_tables())
    w_in_b = w_in.astype(BF16)
    w_s_b = w_s.astype(BF16)
    wpa_b = w_proj_a.astype(BF16)
    wpb_b = w_proj_b.astype(BF16)
    wout_b = w_out.astype(BF16)
    bs_full = jnp.repeat(jnp.swapaxes(b_s, 1, 2), CHUNK, axis=2)
    gfin = g_final.reshape(1, dm)
    g_norm3 = g_norm.reshape(DEPTH, 1, dm)
    g_v3 = g_v.reshape(DEPTH, 1, dm)
    for layer in range(DEPTH):
        proj = _inproj(x2d, g_norm3, w_in_b, layer)
        att = _attn(proj.reshape(bsz, seq, IN_COLS), bias)
        x2d = _mix(proj, att.reshape(m, dm), x2d, w_s_b, bs_full, g_v3, wpa_b, wpb_b, wout_b,
                   gfin, layer, final_norm=(layer == DEPTH - 1))
    return x2d.reshape(bsz, seq, dm)
```

```python
import functools
import math

import numpy as np
import jax
import jax.numpy as jnp
from jax import lax
from jax.experimental import pallas as pl
from jax.experimental.pallas import tpu as pltpu

D_MODEL = 1024
DEPTH = 4
CHUNK = 128
A_GROUPS = 8
N_HEADS = 16
HEAD_DIM = 64
DILATIONS = (1, 4, 16)
BLOCK = 128
EPS = 1e-6
NEG_INF = -1e30
N_COL_BLOCKS = 9
IN_COLS = N_COL_BLOCKS * D_MODEL

LANES = 128
MXU_WIDTH = 256
VMEM_LIMIT = 56 * 1024 * 1024

TM_IN = 1024
TM_MIX = 512

F32 = jnp.float32
BF16 = jnp.bfloat16


_LOG2E = math.log2(math.e)
_GELU_C1 = math.sqrt(2.0 / math.pi)
_GELU_C3 = _GELU_C1 * 0.044715
_Q_SCALE = HEAD_DIM ** -0.5 * _LOG2E


def _gelu(y):
    return (0.5 * y) * (1.0 + jnp.tanh(y * (_GELU_C1 + _GELU_C3 * (y * y))))


def _silu(y):
    h = 0.5 * y
    return h * (1.0 + jnp.tanh(h))


def _sigmoid(y):
    return 0.5 + 0.5 * jnp.tanh(0.5 * y)


def _inproj_kernel(x_ref, g_ref, w_ref, o_ref, h_ref, wb_ref):
    i = pl.program_id(0)
    j = pl.program_id(1)

    @pl.when(j == 0)
    def _():
        x = x_ref[...]
        ms = jnp.mean(x * x, axis=-1, keepdims=True)
        h_ref[...] = (x * lax.rsqrt(ms + EPS) * g_ref[...]).astype(BF16)

    @pl.when(i == 0)
    def _():
        wb_ref[j] = w_ref[...].astype(BF16)

    def project(act):
        for c in range(0, D_MODEL, MXU_WIDTH):
            cs = slice(c, c + MXU_WIDTH)
            y = jnp.dot(h_ref[...], wb_ref[j, :, cs], preferred_element_type=F32)
            o_ref[:, cs] = act(y).astype(BF16)

    @pl.when(j < 2)
    def _():
        project(_gelu)

    @pl.when((j == 2) | (j == 6))
    def _():
        project(_silu)

    @pl.when((j >= 3) & (j <= 5))
    def _():
        scale = jnp.where(j == 3, _Q_SCALE, 1.0).astype(F32)
        project(lambda y: y * scale)

    @pl.when(j >= 7)
    def _():
        project(_sigmoid)


def _inproj(x2d, g_all, w_all, layer):
    m = x2d.shape[0]
    grid = (m // TM_IN, N_COL_BLOCKS)
    last = N_COL_BLOCKS - 1

    def w_index(i, j):
        return (layer, 0, jnp.where(i == 0, j, last))

    return pl.pallas_call(
        _inproj_kernel,
        name="inproj",
        grid=grid,
        in_specs=[
            pl.BlockSpec((TM_IN, D_MODEL), lambda i, j: (i, 0)),
            pl.BlockSpec((None, 1, D_MODEL), lambda i, j: (layer, 0, 0)),
            pl.BlockSpec((None, D_MODEL, D_MODEL), w_index),
        ],
        out_specs=pl.BlockSpec((TM_IN, D_MODEL), lambda i, j: (i, j)),
        scratch_shapes=[pltpu.VMEM((TM_IN, D_MODEL), BF16),
                        pltpu.VMEM((N_COL_BLOCKS, D_MODEL, D_MODEL), BF16)],
        out_shape=jax.ShapeDtypeStruct((m, IN_COLS), BF16),
        compiler_params=pltpu.CompilerParams(
            dimension_semantics=("arbitrary", "arbitrary"),
            vmem_limit_bytes=VMEM_LIMIT),
    )(x2d, g_all, w_all)


def _attn_bias_tables():
    slopes = 2.0 ** (-8.0 * np.arange(1, N_HEADS + 1, dtype=np.float64) / N_HEADS)
    qpos = np.arange(BLOCK)[:, None] + BLOCK
    kpos = np.arange(2 * BLOCK)[None, :]
    dist = qpos - kpos
    valid = (dist >= 0) & (dist <= BLOCK)
    out = np.zeros((N_HEADS // 2, len(DILATIONS), 2 * BLOCK, 2 * BLOCK), np.float32)
    for hp in range(N_HEADS // 2):
        for p, dil in enumerate(DILATIONS):
            for h in range(2):
                pen = -slopes[2 * hp + h] * (dist * dil) * _LOG2E
                out[hp, p, h * BLOCK:(h + 1) * BLOCK] = np.where(valid, pen, NEG_INF)
    return out


def _attn_kernel(q_ref, k_ref, v_ref, bias_ref, o_ref,
                 qf, kf, vf, acc2, l2, m2, acc3, l3, m3):
    seq = q_ref.shape[1]
    qf[...] = q_ref[0].astype(F32)
    kf[...] = k_ref[0].astype(F32)
    vf[...] = v_ref[0].astype(F32)

    lane = lax.broadcasted_iota(jnp.int32, (BLOCK, LANES), 1)
    head0 = lane < HEAD_DIM
    ones_cols = jnp.ones((2 * BLOCK, LANES), BF16)

    def rows(ref, start, size, stride):
        if stride == 1:
            return ref[pl.ds(start, size), :]
        return ref[pl.ds(start, size, stride=stride), :]

    def block(p, qs, ks, nk):
        stride = DILATIONS[p]
        q = rows(qf, qs, BLOCK, stride)
        k = rows(kf, ks, nk, stride).astype(BF16)
        v = rows(vf, ks, nk, stride).astype(BF16)
        q2 = jnp.concatenate([jnp.where(head0, q, 0.0), jnp.where(head0, 0.0, q)],
                             axis=0).astype(BF16)
        s = lax.dot_general(q2, k, (((1,), (1,)), ((), ())),
                            preferred_element_type=F32)
        if nk == 2 * BLOCK:
            s = s + bias_ref[0, p]
        else:
            s = s + bias_ref[0, p, :, BLOCK:]
        m = jnp.max(s, axis=-1, keepdims=True)
        pr = jnp.exp2(s - m).astype(BF16)
        vext = jnp.concatenate([v, ones_cols[:nk]], axis=1)
        r = jnp.dot(pr, vext, preferred_element_type=F32)
        acc = jnp.where(head0, r[:BLOCK, :LANES], r[BLOCK:, :LANES])
        l = jnp.where(head0, r[:BLOCK, LANES:], r[BLOCK:, LANES:])
        mb = jnp.where(head0, jnp.broadcast_to(m[:BLOCK], (BLOCK, LANES)),
                       jnp.broadcast_to(m[BLOCK:], (BLOCK, LANES)))
        return acc, l, mb

    def scatter(p, refs, qs, vals):
        stride = DILATIONS[p]
        for ref, val in zip(refs, vals):
            ref[pl.ds(qs, BLOCK, stride=stride), :] = val

    def pattern_blocks(p):
        d = DILATIONS[p]
        out = []
        for n in range(seq // d // BLOCK):
            for r in range(d):
                qs = r + d * BLOCK * n
                if n == 0:
                    out.append((qs, block(p, qs, qs, BLOCK)))
                else:
                    out.append((qs, block(p, qs, qs - d * BLOCK, 2 * BLOCK)))
        return out

    for p, refs in ((2, (acc3, l3, m3)), (1, (acc2, l2, m2))):
        for qs, val in pattern_blocks(p):
            scatter(p, refs, qs, val)

    def finish(qs, vals):
        acc1, l1, m1 = vals
        a2 = acc2[pl.ds(qs, BLOCK), :]
        a3 = acc3[pl.ds(qs, BLOCK), :]
        ll2 = l2[pl.ds(qs, BLOCK), :]
        ll3 = l3[pl.ds(qs, BLOCK), :]
        mm2 = m2[pl.ds(qs, BLOCK), :]
        mm3 = m3[pl.ds(qs, BLOCK), :]
        mx = jnp.maximum(jnp.maximum(m1, mm2), mm3)
        e1 = jnp.exp2(m1 - mx)
        e2 = jnp.exp2(mm2 - mx)
        e3 = jnp.exp2(mm3 - mx)
        num = e1 * acc1 + e2 * a2 + e3 * a3
        den = e1 * l1 + e2 * ll2 + e3 * ll3
        o_ref[0, pl.ds(qs, BLOCK), :] = (num / den).astype(o_ref.dtype)

    for qs, val in pattern_blocks(0):
        finish(qs, val)


def _attn(proj3d, bias):
    bsz, seq, _ = proj3d.shape
    n_pairs = N_HEADS // 2
    col0 = 3 * D_MODEL // LANES
    per = D_MODEL // LANES
    blk = (1, seq, LANES)
    stat = pltpu.VMEM((seq, LANES), F32)
    return pl.pallas_call(
        _attn_kernel,
        name="attn",
        grid=(n_pairs, bsz),
        in_specs=[
            pl.BlockSpec(blk, lambda hp, b: (b, 0, col0 + hp)),
            pl.BlockSpec(blk, lambda hp, b: (b, 0, col0 + per + hp)),
            pl.BlockSpec(blk, lambda hp, b: (b, 0, col0 + 2 * per + hp)),
            pl.BlockSpec((1, len(DILATIONS), 2 * BLOCK, 2 * BLOCK),
                         lambda hp, b: (hp, 0, 0, 0)),
        ],
        out_specs=pl.BlockSpec(blk, lambda hp, b: (b, 0, hp)),
        out_shape=jax.ShapeDtypeStruct((bsz, seq, D_MODEL), BF16),
        scratch_shapes=[stat] * 9,
        compiler_params=pltpu.CompilerParams(
            dimension_semantics=("arbitrary", "arbitrary"),
            vmem_limit_bytes=VMEM_LIMIT),
    )(proj3d, proj3d, proj3d, bias)


def _mix_kernel(u_ref, gv_ref, gate_ref, att_ref, bgate_ref, sa_ref, sb_ref, x_ref,
                ws_ref, bs_ref, gvw_ref, wpa_ref, wpb_ref, wout_ref, gfin_ref,
                out_ref, ya_ref, *, final_norm):
    tm = x_ref.shape[0]
    gv = gv_ref[...].astype(F32)
    ms = jnp.mean(gv * gv, axis=-1, keepdims=True)
    vn = (gv * lax.rsqrt(ms + EPS) * gvw_ref[...]).astype(BF16)

    row = lax.broadcasted_iota(jnp.int32, (CHUNK, CHUNK), 0)
    col = lax.broadcasted_iota(jnp.int32, (CHUNK, CHUNK), 1)
    causal = col <= row
    for g in range(A_GROUPS):
        w = jnp.where(causal, ws_ref[g], jnp.zeros((), BF16))
        cs = slice(g * CHUNK, (g + 1) * CHUNK)
        for c in range(tm // CHUNK):
            rs = slice(c * CHUNK, (c + 1) * CHUNK)
            mixed = jnp.dot(w, vn[rs, cs], preferred_element_type=F32) + bs_ref[:, cs]
            ya = u_ref[rs, cs].astype(F32) * mixed * gate_ref[rs, cs].astype(F32)
            ya_ref[rs, cs] = ya.astype(BF16)

    za = jnp.dot(ya_ref[...], wpa_ref[...], preferred_element_type=F32)
    yb = (att_ref[...].astype(F32) * bgate_ref[...].astype(F32)).astype(BF16)
    zb = jnp.dot(yb, wpb_ref[...], preferred_element_type=F32)
    merged = (sa_ref[...].astype(F32) * za + sb_ref[...].astype(F32) * zb).astype(BF16)
    xn = x_ref[...] + jnp.dot(merged, wout_ref[...], preferred_element_type=F32)
    if final_norm:
        ms2 = jnp.mean(xn * xn, axis=-1, keepdims=True)
        xn = xn * lax.rsqrt(ms2 + EPS) * gfin_ref[...]
    out_ref[...] = xn


def _mix(proj, att2d, x2d, ws_bf16, bs_full, gvw, wpa, wpb, wout, gfin, layer, final_norm):
    m = x2d.shape[0]
    tile = (TM_MIX, D_MODEL)

    def col_block(jc):
        return pl.BlockSpec(tile, lambda i: (i, jc))

    def whole(shape):
        return pl.BlockSpec((None,) + shape, lambda i: (layer,) + (0,) * len(shape))

    return pl.pallas_call(
        functools.partial(_mix_kernel, final_norm=final_norm),
        name="mix",
        grid=(m // TM_MIX,),
        in_specs=[
            col_block(0), col_block(1), col_block(2),
            col_block(0),
            col_block(6), col_block(7), col_block(8),
            col_block(0),
            whole((A_GROUPS, CHUNK, CHUNK)),
            whole((CHUNK, D_MODEL)),
            whole((1, D_MODEL)),
            whole((D_MODEL, D_MODEL)), whole((D_MODEL, D_MODEL)), whole((D_MODEL, D_MODEL)),
            pl.BlockSpec((1, D_MODEL), lambda i: (0, 0)),
        ],
        out_specs=col_block(0),
        out_shape=jax.ShapeDtypeStruct((m, D_MODEL), F32),
        scratch_shapes=[pltpu.VMEM(tile, BF16)],
        compiler_params=pltpu.CompilerParams(
            dimension_semantics=("arbitrary",),
            vmem_limit_bytes=VMEM_LIMIT),
    )(proj, proj, proj, att2d, proj, proj, proj, x2d,
      ws_bf16, bs_full, gvw, wpa, wpb, wout, gfin)


def kernel(x, g_norm, w_in, w_s, b_s, g_v, w_proj_a, w_proj_b, w_out, g_final):
    bsz, seq, dm = x.shape
    assert dm == D_MODEL and seq % (DILATIONS[-1] * BLOCK) == 0
    assert (bsz * seq) % TM_IN == 0 and (bsz * seq) % TM_MIX == 0
    m = bsz * seq
    x2d = x.reshape(m, dm)
    bias = jnp.asarray(_attn_bias_tables())
    w_s_b = w_s.astype(BF16)
    wpa_b = w_proj_a.astype(BF16)
    wpb_b = w_proj_b.astype(BF16)
    wout_b = w_out.astype(BF16)
    bs_full = jnp.repeat(jnp.swapaxes(b_s, 1, 2), CHUNK, axis=2)
    gfin = g_final.reshape(1, dm)
    g_norm3 = g_norm.reshape(DEPTH, 1, dm)
    g_v3 = g_v.reshape(DEPTH, 1, dm)
    for layer in range(DEPTH):
        proj = _inproj(x2d, g_norm3, w_in, layer)
        att = _attn(proj.reshape(bsz, seq, IN_COLS), bias)
        x2d = _mix(proj, att.reshape(m, dm), x2d, w_s_b, bs_full, g_v3, wpa_b, wpb_b, wout_b,
                   gfin, layer, final_norm=(layer == DEPTH - 1))
    return x2d.reshape(bsz, seq, dm)
```

```python
import functools
import math

import numpy as np
import jax
import jax.numpy as jnp
from jax import lax
from jax.experimental import pallas as pl
from jax.experimental.pallas import tpu as pltpu

D_MODEL = 1024
DEPTH = 4
CHUNK = 128
A_GROUPS = 8
N_HEADS = 16
HEAD_DIM = 64
DILATIONS = (1, 4, 16)
BLOCK = 128
EPS = 1e-6
NEG_INF = -1e30
N_COL_BLOCKS = 9
IN_COLS = N_COL_BLOCKS * D_MODEL

LANES = 128
MXU_WIDTH = 256
VMEM_LIMIT = 56 * 1024 * 1024

TM_IN = 1024
TM_MIX = 512

F32 = jnp.float32
BF16 = jnp.bfloat16


_LOG2E = math.log2(math.e)
_GELU_C1 = math.sqrt(2.0 / math.pi)
_GELU_C3 = _GELU_C1 * 0.044715
_Q_SCALE = HEAD_DIM ** -0.5 * _LOG2E


def _gelu(y):
    return (0.5 * y) * (1.0 + jnp.tanh(y * (_GELU_C1 + _GELU_C3 * (y * y))))


def _silu(y):
    h = 0.5 * y
    return h * (1.0 + jnp.tanh(h))


def _sigmoid(y):
    return 0.5 + 0.5 * jnp.tanh(0.5 * y)


def _inproj_kernel(x_ref, g_ref, w_ref, o_ref, h_ref, wb_ref):
    i = pl.program_id(0)
    j = pl.program_id(1)

    @pl.when(j == 0)
    def _():
        x = x_ref[...]
        ms = jnp.mean(x * x, axis=-1, keepdims=True)
        h_ref[...] = (x * lax.rsqrt(ms + EPS) * g_ref[...]).astype(BF16)

    @pl.when(i == 0)
    def _():
        wb_ref[j] = w_ref[...].astype(BF16)

    def project(act):
        for c in range(0, D_MODEL, MXU_WIDTH):
            cs = slice(c, c + MXU_WIDTH)
            for r in range(0, TM_IN, TM_IN // 2):
                rs = slice(r, r + TM_IN // 2)
                y = jnp.dot(h_ref[rs, :], wb_ref[j, :, cs], preferred_element_type=F32)
                o_ref[rs, cs] = act(y).astype(BF16)

    @pl.when(j < 2)
    def _():
        project(_gelu)

    @pl.when((j == 2) | (j == 6))
    def _():
        project(_silu)

    @pl.when((j >= 3) & (j <= 5))
    def _():
        scale = jnp.where(j == 3, _Q_SCALE, 1.0).astype(F32)
        project(lambda y: y * scale)

    @pl.when(j >= 7)
    def _():
        project(_sigmoid)


def _inproj(x2d, g_all, w_all, layer):
    m = x2d.shape[0]
    grid = (m // TM_IN, N_COL_BLOCKS)
    last = N_COL_BLOCKS - 1

    def w_index(i, j):
        return (layer, 0, jnp.where(i == 0, j, last))

    return pl.pallas_call(
        _inproj_kernel,
        name="inproj",
        grid=grid,
        in_specs=[
            pl.BlockSpec((TM_IN, D_MODEL), lambda i, j: (i, 0)),
            pl.BlockSpec((None, 1, D_MODEL), lambda i, j: (layer, 0, 0)),
            pl.BlockSpec((None, D_MODEL, D_MODEL), w_index),
        ],
        out_specs=pl.BlockSpec((TM_IN, D_MODEL), lambda i, j: (i, j)),
        scratch_shapes=[pltpu.VMEM((TM_IN, D_MODEL), BF16),
                        pltpu.VMEM((N_COL_BLOCKS, D_MODEL, D_MODEL), BF16)],
        out_shape=jax.ShapeDtypeStruct((m, IN_COLS), BF16),
        compiler_params=pltpu.CompilerParams(
            dimension_semantics=("arbitrary", "arbitrary"),
            vmem_limit_bytes=VMEM_LIMIT),
    )(x2d, g_all, w_all)


def _attn_bias_tables():
    slopes = 2.0 ** (-8.0 * np.arange(1, N_HEADS + 1, dtype=np.float64) / N_HEADS)
    qpos = np.arange(BLOCK)[:, None] + BLOCK
    kpos = np.arange(2 * BLOCK)[None, :]
    dist = qpos - kpos
    valid = (dist >= 0) & (dist <= BLOCK)
    out = np.zeros((N_HEADS // 2, len(DILATIONS), 2 * BLOCK, 2 * BLOCK), np.float32)
    for hp in range(N_HEADS // 2):
        for p, dil in enumerate(DILATIONS):
            for h in range(2):
                pen = -slopes[2 * hp + h] * (dist * dil) * _LOG2E
                out[hp, p, h * BLOCK:(h + 1) * BLOCK] = np.where(valid, pen, NEG_INF)
    return out


def _attn_kernel(q_ref, k_ref, v_ref, bias_ref, o_ref,
                 qf, kf, vf, q4, k4, v4, acc2, l2, m2, acc3g, l3g, m3g, acc3, l3, m3):
    seq = q_ref.shape[1]
    d2, d3 = DILATIONS[1], DILATIONS[2]
    assert d3 == d2 * d2
    n4 = seq // d2

    qf[...] = q_ref[0].astype(F32)
    kf[...] = k_ref[0].astype(F32)
    vf[...] = v_ref[0].astype(F32)
    for src, dst in ((qf, q4), (kf, k4), (vf, v4)):
        for r in range(d2):
            dst[r * n4:(r + 1) * n4, :] = src[pl.ds(r, n4, stride=d2), :]

    lane = lax.broadcasted_iota(jnp.int32, (BLOCK, LANES), 1)
    head0 = lane < HEAD_DIM
    ones_cols = jnp.ones((2 * BLOCK, LANES), BF16)

    def block(p, q, k, v):
        nk = k.shape[0]
        q2 = jnp.concatenate([jnp.where(head0, q, 0.0), jnp.where(head0, 0.0, q)],
                             axis=0).astype(BF16)
        s = lax.dot_general(q2, k.astype(BF16), (((1,), (1,)), ((), ())),
                            preferred_element_type=F32)
        if nk == 2 * BLOCK:
            s = s + bias_ref[0, p]
        else:
            s = s + bias_ref[0, p, :, BLOCK:]
        m = jnp.max(s, axis=-1, keepdims=True)
        pr = jnp.exp2(s - m).astype(BF16)
        vext = jnp.concatenate([v.astype(BF16), ones_cols[:nk]], axis=1)
        r = jnp.dot(pr, vext, preferred_element_type=F32)
        acc = jnp.where(head0, r[:BLOCK, :LANES], r[BLOCK:, :LANES])
        l = jnp.where(head0, r[:BLOCK, LANES:], r[BLOCK:, LANES:])
        mb = jnp.where(head0, jnp.broadcast_to(m[:BLOCK], (BLOCK, LANES)),
                       jnp.broadcast_to(m[BLOCK:], (BLOCK, LANES)))
        return acc, l, mb


    for a in range(d2):
        for b in range(d2):
            rows = pl.ds(a * n4 + b, BLOCK, stride=d2)
            vals = block(2, q4[rows, :], k4[rows, :], v4[rows, :])
            for ref, val in zip((acc3g, l3g, m3g), vals):
                ref[rows, :] = val
    for src, dst in ((acc3g, acc3), (l3g, l3), (m3g, m3)):
        for r in range(d2):
            dst[pl.ds(r, n4, stride=d2), :] = src[r * n4:(r + 1) * n4, :]

    for n in range(n4 // BLOCK):
        for r in range(d2):
            g0 = r * n4 + n * BLOCK
            k0 = g0 - (BLOCK if n else 0)
            vals = block(1, q4[g0:g0 + BLOCK, :], k4[k0:g0 + BLOCK, :], v4[k0:g0 + BLOCK, :])
            for ref, val in zip((acc2, l2, m2), vals):
                ref[pl.ds(r + d2 * BLOCK * n, BLOCK, stride=d2), :] = val

    for n in range(seq // BLOCK):
        qs = n * BLOCK
        k0 = qs - (BLOCK if n else 0)
        acc1, l1, m1 = block(0, qf[qs:qs + BLOCK, :], kf[k0:qs + BLOCK, :],
                             vf[k0:qs + BLOCK, :])
        rows = slice(qs, qs + BLOCK)
        mm2, mm3 = m2[rows, :], m3[rows, :]
        mx = jnp.maximum(jnp.maximum(m1, mm2), mm3)
        e1 = jnp.exp2(m1 - mx)
        e2 = jnp.exp2(mm2 - mx)
        e3 = jnp.exp2(mm3 - mx)
        num = e1 * acc1 + e2 * acc2[rows, :] + e3 * acc3[rows, :]
        den = e1 * l1 + e2 * l2[rows, :] + e3 * l3[rows, :]
        o_ref[0, rows, :] = (num / den).astype(o_ref.dtype)


def _attn(proj3d, bias):
    bsz, seq, _ = proj3d.shape
    n_pairs = N_HEADS // 2
    col0 = 3 * D_MODEL // LANES
    per = D_MODEL // LANES
    blk = (1, seq, LANES)
    stat = pltpu.VMEM((seq, LANES), F32)
    return pl.pallas_call(
        _attn_kernel,
        name="attn",
        grid=(n_pairs, bsz),
        in_specs=[
            pl.BlockSpec(blk, lambda hp, b: (b, 0, col0 + hp)),
            pl.BlockSpec(blk, lambda hp, b: (b, 0, col0 + per + hp)),
            pl.BlockSpec(blk, lambda hp, b: (b, 0, col0 + 2 * per + hp)),
            pl.BlockSpec((1, len(DILATIONS), 2 * BLOCK, 2 * BLOCK),
                         lambda hp, b: (hp, 0, 0, 0)),
        ],
        out_specs=pl.BlockSpec(blk, lambda hp, b: (b, 0, hp)),
        out_shape=jax.ShapeDtypeStruct((bsz, seq, D_MODEL), BF16),
        scratch_shapes=[stat] * 15,
        compiler_params=pltpu.CompilerParams(
            dimension_semantics=("arbitrary", "arbitrary"),
            vmem_limit_bytes=VMEM_LIMIT),
    )(proj3d, proj3d, proj3d, bias)


def _mix_kernel(u_ref, gv_ref, gate_ref, att_ref, bgate_ref, sa_ref, sb_ref, x_ref,
                ws_ref, bs_ref, gvw_ref, wpa_ref, wpb_ref, wout_ref, gfin_ref,
                out_ref, ya_ref, *, final_norm):
    tm = x_ref.shape[0]
    gv = gv_ref[...].astype(F32)
    ms = jnp.mean(gv * gv, axis=-1, keepdims=True)
    vn = (gv * lax.rsqrt(ms + EPS) * gvw_ref[...]).astype(BF16)

    row = lax.broadcasted_iota(jnp.int32, (CHUNK, CHUNK), 0)
    col = lax.broadcasted_iota(jnp.int32, (CHUNK, CHUNK), 1)
    causal = col <= row
    for g in range(A_GROUPS):
        w = jnp.where(causal, ws_ref[g], jnp.zeros((), BF16))
        cs = slice(g * CHUNK, (g + 1) * CHUNK)
        for c in range(tm // CHUNK):
            rs = slice(c * CHUNK, (c + 1) * CHUNK)
            mixed = jnp.dot(w, vn[rs, cs], preferred_element_type=F32) + bs_ref[:, cs]
            ya = u_ref[rs, cs].astype(F32) * mixed * gate_ref[rs, cs].astype(F32)
            ya_ref[rs, cs] = ya.astype(BF16)

    za = jnp.dot(ya_ref[...], wpa_ref[...], preferred_element_type=F32)
    yb = (att_ref[...].astype(F32) * bgate_ref[...].astype(F32)).astype(BF16)
    zb = jnp.dot(yb, wpb_ref[...], preferred_element_type=F32)
    merged = (sa_ref[...].astype(F32) * za + sb_ref[...].astype(F32) * zb).astype(BF16)
    xn = x_ref[...] + jnp.dot(merged, wout_ref[...], preferred_element_type=F32)
    if final_norm:
        ms2 = jnp.mean(xn * xn, axis=-1, keepdims=True)
        xn = xn * lax.rsqrt(ms2 + EPS) * gfin_ref[...]
    out_ref[...] = xn


def _mix(proj, att2d, x2d, ws_bf16, bs_full, gvw, wpa, wpb, wout, gfin, layer, final_norm):
    m = x2d.shape[0]
    tile = (TM_MIX, D_MODEL)

    def col_block(jc):
        return pl.BlockSpec(tile, lambda i: (i, jc))

    def whole(shape):
        return pl.BlockSpec((None,) + shape, lambda i: (layer,) + (0,) * len(shape))

    return pl.pallas_call(
        functools.partial(_mix_kernel, final_norm=final_norm),
        name="mix",
        grid=(m // TM_MIX,),
        in_specs=[
            col_block(0), col_block(1), col_block(2),
            col_block(0),
            col_block(6), col_block(7), col_block(8),
            col_block(0),
            whole((A_GROUPS, CHUNK, CHUNK)),
            whole((CHUNK, D_MODEL)),
            whole((1, D_MODEL)),
            whole((D_MODEL, D_MODEL)), whole((D_MODEL, D_MODEL)), whole((D_MODEL, D_MODEL)),
            pl.BlockSpec((1, D_MODEL), lambda i: (0, 0)),
        ],
        out_specs=col_block(0),
        out_shape=jax.ShapeDtypeStruct((m, D_MODEL), F32),
        scratch_shapes=[pltpu.VMEM(tile, BF16)],
        compiler_params=pltpu.CompilerParams(
            dimension_semantics=("arbitrary",),
            vmem_limit_bytes=VMEM_LIMIT),
    )(proj, proj, proj, att2d, proj, proj, proj, x2d,
      ws_bf16, bs_full, gvw, wpa, wpb, wout, gfin)


def kernel(x, g_norm, w_in, w_s, b_s, g_v, w_proj_a, w_proj_b, w_out, g_final):
    bsz, seq, dm = x.shape
    assert dm == D_MODEL and seq % (DILATIONS[-1] * BLOCK) == 0
    assert (bsz * seq) % TM_IN == 0 and (bsz * seq) % TM_MIX == 0
    m = bsz * seq
    x2d = x.reshape(m, dm)
    bias = jnp.asarray(_attn_bias_tables())
    w_s_b = w_s.astype(BF16)
    wpa_b = w_proj_a.astype(BF16)
    wpb_b = w_proj_b.astype(BF16)
    wout_b = w_out.astype(BF16)
    bs_full = jnp.repeat(jnp.swapaxes(b_s, 1, 2), CHUNK, axis=2)
    gfin = g_final.reshape(1, dm)
    g_norm3 = g_norm.reshape(DEPTH, 1, dm)
    g_v3 = g_v.reshape(DEPTH, 1, dm)
    for layer in range(DEPTH):
        proj = _inproj(x2d, g_norm3, w_in, layer)
        att = _attn(proj.reshape(bsz, seq, IN_COLS), bias)
        x2d = _mix(proj, att.reshape(m, dm), x2d, w_s_b, bs_full, g_v3, wpa_b, wpb_b, wout_b,
                   gfin, layer, final_norm=(layer == DEPTH - 1))
    return x2d.reshape(bsz, seq, dm)
```

```python
import functools
import math

import numpy as np
import jax
import jax.numpy as jnp
from jax import lax
from jax.experimental import pallas as pl
from jax.experimental.pallas import tpu as pltpu

D_MODEL = 1024
DEPTH = 4
CHUNK = 128
A_GROUPS = 8
N_HEADS = 16
HEAD_DIM = 64
DILATIONS = (1, 4, 16)
BLOCK = 128
EPS = 1e-6
NEG_INF = -1e30
N_COL_BLOCKS = 9
IN_COLS = N_COL_BLOCKS * D_MODEL

LANES = 128
MXU_WIDTH = 256
VMEM_LIMIT = 56 * 1024 * 1024

TM_IN = 1024
TM_MIX = 512

F32 = jnp.float32
BF16 = jnp.bfloat16


_LOG2E = math.log2(math.e)
_GELU_C1 = math.sqrt(2.0 / math.pi)
_GELU_C3 = _GELU_C1 * 0.044715
_Q_SCALE = HEAD_DIM ** -0.5 * _LOG2E


def _gelu(y):
    return (0.5 * y) * (1.0 + jnp.tanh(y * (_GELU_C1 + _GELU_C3 * (y * y))))


def _silu(y):
    h = 0.5 * y
    return h * (1.0 + jnp.tanh(h))


def _sigmoid(y):
    return 0.5 + 0.5 * jnp.tanh(0.5 * y)


def _inproj_kernel(x_ref, g_ref, w_ref, o_ref, h_ref, wb_ref):
    i = pl.program_id(0)
    j = pl.program_id(1)

    @pl.when(j == 0)
    def _():
        x = x_ref[...]
        ms = jnp.mean(x * x, axis=-1, keepdims=True)
        h_ref[...] = (x * lax.rsqrt(ms + EPS) * g_ref[...]).astype(BF16)

    @pl.when(i == 0)
    def _():
        wb_ref[j] = w_ref[...].astype(BF16)

    def project(act):
        for c in range(0, D_MODEL, MXU_WIDTH):
            cs = slice(c, c + MXU_WIDTH)
            y = jnp.dot(h_ref[...], wb_ref[j, :, cs], preferred_element_type=F32)
            o_ref[:, cs] = act(y).astype(BF16)

    @pl.when(j < 2)
    def _():
        project(_gelu)

    @pl.when((j == 2) | (j == 6))
    def _():
        project(_silu)

    @pl.when((j >= 3) & (j <= 5))
    def _():
        scale = jnp.where(j == 3, _Q_SCALE, 1.0).astype(F32)
        project(lambda y: y * scale)

    @pl.when(j >= 7)
    def _():
        project(_sigmoid)


def _inproj(x2d, g_all, w_all, layer):
    m = x2d.shape[0]
    grid = (m // TM_IN, N_COL_BLOCKS)
    last = N_COL_BLOCKS - 1

    def w_index(i, j):
        return (layer, 0, jnp.where(i == 0, j, last))

    return pl.pallas_call(
        _inproj_kernel,
        name="inproj",
        grid=grid,
        in_specs=[
            pl.BlockSpec((TM_IN, D_MODEL), lambda i, j: (i, 0)),
            pl.BlockSpec((None, 1, D_MODEL), lambda i, j: (layer, 0, 0)),
            pl.BlockSpec((None, D_MODEL, D_MODEL), w_index),
        ],
        out_specs=pl.BlockSpec((TM_IN, D_MODEL), lambda i, j: (i, j)),
        scratch_shapes=[pltpu.VMEM((TM_IN, D_MODEL), BF16),
                        pltpu.VMEM((N_COL_BLOCKS, D_MODEL, D_MODEL), BF16)],
        out_shape=jax.ShapeDtypeStruct((m, IN_COLS), BF16),
        compiler_params=pltpu.CompilerParams(
            dimension_semantics=("arbitrary", "arbitrary"),
            vmem_limit_bytes=VMEM_LIMIT),
    )(x2d, g_all, w_all)


def _attn_bias_tables():
    slopes = 2.0 ** (-8.0 * np.arange(1, N_HEADS + 1, dtype=np.float64) / N_HEADS)
    qpos = np.arange(BLOCK)[:, None] + BLOCK
    kpos = np.arange(2 * BLOCK)[None, :]
    dist = qpos - kpos
    valid = (dist >= 0) & (dist <= BLOCK)
    out = np.zeros((N_HEADS // 2, len(DILATIONS), 2, 2 * BLOCK, 2 * BLOCK), np.float32)
    for hp in range(N_HEADS // 2):
        for p, dil in enumerate(DILATIONS):
            for h in range(2):
                pen = -slopes[2 * hp + h] * (dist * dil) * _LOG2E
                rows = slice(h * BLOCK, (h + 1) * BLOCK)
                out[hp, p, 0, rows] = np.where(valid, pen, NEG_INF)
                out[hp, p, 1, rows] = np.where(valid & (kpos >= BLOCK), pen, NEG_INF)
    return out


def _attn_kernel(q_ref, k_ref, v_ref, bias_ref, o_ref,
                 qf, kf, vf, q4, k4, v4, q16, k16, v16,
                 acc2, l2, m2, acc3c, l3c, m3c, acc3g, l3g, m3g, acc3, l3, m3):
    seq = q_ref.shape[1]
    d2, d3 = DILATIONS[1], DILATIONS[2]
    assert d3 == d2 * d2
    n4 = seq // d2

    qf[...] = q_ref[0].astype(F32)
    kf[...] = k_ref[0].astype(F32)
    vf[...] = v_ref[0].astype(F32)
    for src, dst in ((qf, q4), (kf, k4), (vf, v4)):
        for r in range(d2):
            dst[r * n4:(r + 1) * n4, :] = src[pl.ds(r, n4, stride=d2), :]
    for src, dst in ((q4, q16), (k4, k16), (v4, v16)):
        for a in range(d2):
            for b in range(d2):
                c0 = (a * d2 + b) * BLOCK
                dst[c0:c0 + BLOCK, :] = src[pl.ds(a * n4 + b, BLOCK, stride=d2), :].astype(BF16)

    lane = lax.broadcasted_iota(jnp.int32, (BLOCK, LANES), 1)
    head0 = lane < HEAD_DIM
    ones_cols = jnp.ones((2 * BLOCK, LANES), BF16)

    def block(p, src, g0, first):
        q_src, k_src, v_src = src
        q = q_src[g0:g0 + BLOCK, :]
        if g0 == 0:
            k = jnp.concatenate([k_src[:BLOCK, :]] * 2, axis=0)
            v = jnp.concatenate([v_src[:BLOCK, :]] * 2, axis=0)
        else:
            k = k_src[g0 - BLOCK:g0 + BLOCK, :]
            v = v_src[g0 - BLOCK:g0 + BLOCK, :]
        zero = jnp.zeros_like(q)
        q2 = jnp.concatenate([jnp.where(head0, q, zero), jnp.where(head0, zero, q)],
                             axis=0).astype(BF16)
        s = lax.dot_general(q2, k.astype(BF16), (((1,), (1,)), ((), ())),
                            preferred_element_type=F32)
        s = s + bias_ref[0, p, int(first)]
        m = jnp.max(s, axis=-1, keepdims=True)
        pr = jnp.exp2(s - m).astype(BF16)
        vext = jnp.concatenate([v.astype(BF16), ones_cols], axis=1)
        r = jnp.dot(pr, vext, preferred_element_type=F32)
        acc = jnp.where(head0, r[:BLOCK, :LANES], r[BLOCK:, :LANES])
        l = jnp.where(head0, r[:BLOCK, LANES:], r[BLOCK:, LANES:])
        mb = jnp.where(head0, jnp.broadcast_to(m[:BLOCK], (BLOCK, LANES)),
                       jnp.broadcast_to(m[BLOCK:], (BLOCK, LANES)))
        return acc, l, mb


    for n in range(n4 // BLOCK):
        for r in range(d2):
            vals = block(1, (q4, k4, v4), r * n4 + n * BLOCK, n == 0)
            for ref, val in zip((acc2, l2, m2), vals):
                ref[pl.ds(r + d2 * BLOCK * n, BLOCK, stride=d2), :] = val

    for c0 in range(0, seq, BLOCK):
        vals = block(2, (q16, k16, v16), c0, True)
        for ref, val in zip((acc3c, l3c, m3c), vals):
            ref[c0:c0 + BLOCK, :] = val
    for src, mid, dst in ((acc3c, acc3g, acc3), (l3c, l3g, l3), (m3c, m3g, m3)):
        for a in range(d2):
            for b in range(d2):
                c0 = (a * d2 + b) * BLOCK
                mid[pl.ds(a * n4 + b, BLOCK, stride=d2), :] = src[c0:c0 + BLOCK, :]
        for r in range(d2):
            dst[pl.ds(r, n4, stride=d2), :] = mid[r * n4:(r + 1) * n4, :]

    for n in range(seq // BLOCK):
        qs = n * BLOCK
        acc1, l1, m1 = block(0, (q_ref.at[0], k_ref.at[0], v_ref.at[0]), qs, n == 0)
        rows = slice(qs, qs + BLOCK)
        mm2, mm3 = m2[rows, :], m3[rows, :]
        mx = jnp.maximum(jnp.maximum(m1, mm2), mm3)
        e1 = jnp.exp2(m1 - mx)
        e2 = jnp.exp2(mm2 - mx)
        e3 = jnp.exp2(mm3 - mx)
        num = e1 * acc1 + e2 * acc2[rows, :] + e3 * acc3[rows, :]
        den = e1 * l1 + e2 * l2[rows, :] + e3 * l3[rows, :]
        o_ref[0, rows, :] = (num / den).astype(o_ref.dtype)


def _attn(proj3d, bias):
    bsz, seq, _ = proj3d.shape
    n_pairs = N_HEADS // 2
    col0 = 3 * D_MODEL // LANES
    per = D_MODEL // LANES
    blk = (1, seq, LANES)
    stat = pltpu.VMEM((seq, LANES), F32)
    return pl.pallas_call(
        _attn_kernel,
        name="attn",
        grid=(n_pairs, bsz),
        in_specs=[
            pl.BlockSpec(blk, lambda hp, b: (b, 0, col0 + hp)),
            pl.BlockSpec(blk, lambda hp, b: (b, 0, col0 + per + hp)),
            pl.BlockSpec(blk, lambda hp, b: (b, 0, col0 + 2 * per + hp)),
            pl.BlockSpec((1, len(DILATIONS), 2, 2 * BLOCK, 2 * BLOCK),
                         lambda hp, b: (hp, 0, 0, 0, 0)),
        ],
        out_specs=pl.BlockSpec(blk, lambda hp, b: (b, 0, hp)),
        out_shape=jax.ShapeDtypeStruct((bsz, seq, D_MODEL), BF16),
        scratch_shapes=[stat] * 6 + [pltpu.VMEM((seq, LANES), BF16)] * 3 + [stat] * 12,
        compiler_params=pltpu.CompilerParams(
            dimension_semantics=("arbitrary", "arbitrary"),
            vmem_limit_bytes=VMEM_LIMIT),
    )(proj3d, proj3d, proj3d, bias)


def _mix_kernel(u_ref, gv_ref, gate_ref, att_ref, bgate_ref, sa_ref, sb_ref, x_ref,
                ws_ref, bs_ref, gvw_ref, wpa_ref, wpb_ref, wout_ref, gfin_ref,
                out_ref, ya_ref, *, final_norm):
    tm = x_ref.shape[0]
    gv = gv_ref[...].astype(F32)
    ms = jnp.mean(gv * gv, axis=-1, keepdims=True)
    vn = (gv * lax.rsqrt(ms + EPS) * gvw_ref[...]).astype(BF16)

    row = lax.broadcasted_iota(jnp.int32, (CHUNK, CHUNK), 0)
    col = lax.broadcasted_iota(jnp.int32, (CHUNK, CHUNK), 1)
    causal = col <= row
    for g in range(A_GROUPS):
        w = jnp.where(causal, ws_ref[g], jnp.zeros((), BF16))
        cs = slice(g * CHUNK, (g + 1) * CHUNK)
        for c in range(tm // CHUNK):
            rs = slice(c * CHUNK, (c + 1) * CHUNK)
            mixed = jnp.dot(w, vn[rs, cs], preferred_element_type=F32) + bs_ref[:, cs]
            ya = u_ref[rs, cs].astype(F32) * mixed * gate_ref[rs, cs].astype(F32)
            ya_ref[rs, cs] = ya.astype(BF16)

    za = jnp.dot(ya_ref[...], wpa_ref[...], preferred_element_type=F32)
    yb = (att_ref[...].astype(F32) * bgate_ref[...].astype(F32)).astype(BF16)
    zb = jnp.dot(yb, wpb_ref[...], preferred_element_type=F32)
    merged = (sa_ref[...].astype(F32) * za + sb_ref[...].astype(F32) * zb).astype(BF16)
    xn = x_ref[...] + jnp.dot(merged, wout_ref[...], preferred_element_type=F32)
    if final_norm:
        ms2 = jnp.mean(xn * xn, axis=-1, keepdims=True)
        xn = xn * lax.rsqrt(ms2 + EPS) * gfin_ref[...]
    out_ref[...] = xn


def _mix(proj, att2d, x2d, ws_bf16, bs_full, gvw, wpa, wpb, wout, gfin, layer, final_norm):
    m = x2d.shape[0]
    tile = (TM_MIX, D_MODEL)

    def col_block(jc):
        return pl.BlockSpec(tile, lambda i: (i, jc))

    def whole(shape):
        return pl.BlockSpec((None,) + shape, lambda i: (layer,) + (0,) * len(shape))

    return pl.pallas_call(
        functools.partial(_mix_kernel, final_norm=final_norm),
        name="mix",
        grid=(m // TM_MIX,),
        in_specs=[
            col_block(0), col_block(1), col_block(2),
            col_block(0),
            col_block(6), col_block(7), col_block(8),
            col_block(0),
            whole((A_GROUPS, CHUNK, CHUNK)),
            whole((CHUNK, D_MODEL)),
            whole((1, D_MODEL)),
            whole((D_MODEL, D_MODEL)), whole((D_MODEL, D_MODEL)), whole((D_MODEL, D_MODEL)),
            pl.BlockSpec((1, D_MODEL), lambda i: (0, 0)),
        ],
        out_specs=col_block(0),
        out_shape=jax.ShapeDtypeStruct((m, D_MODEL), F32),
        scratch_shapes=[pltpu.VMEM(tile, BF16)],
        compiler_params=pltpu.CompilerParams(
            dimension_semantics=("arbitrary",),
            vmem_limit_bytes=VMEM_LIMIT),
    )(proj, proj, proj, att2d, proj, proj, proj, x2d,
      ws_bf16, bs_full, gvw, wpa, wpb, wout, gfin)


def kernel(x, g_norm, w_in, w_s, b_s, g_v, w_proj_a, w_proj_b, w_out, g_final):
    bsz, seq, dm = x.shape
    assert dm == D_MODEL and seq % (DILATIONS[-1] * BLOCK) == 0
    assert (bsz * seq) % TM_IN == 0 and (bsz * seq) % TM_MIX == 0
    m = bsz * seq
    x2d = x.reshape(m, dm)
    bias = jnp.asarray(_attn_bias_tables())
    w_s_b = w_s.astype(BF16)
    wpa_b = w_proj_a.astype(BF16)
    wpb_b = w_proj_b.astype(BF16)
    wout_b = w_out.astype(BF16)
    bs_full = jnp.repeat(jnp.swapaxes(b_s, 1, 2), CHUNK, axis=2)
    gfin = g_final.reshape(1, dm)
    g_norm3 = g_norm.reshape(DEPTH, 1, dm)
    g_v3 = g_v.reshape(DEPTH, 1, dm)
    for layer in range(DEPTH):
        proj = _inproj(x2d, g_norm3, w_in, layer)
        att = _attn(proj.reshape(bsz, seq, IN_COLS), bias)
        x2d = _mix(proj, att.reshape(m, dm), x2d, w_s_b, bs_full, g_v3, wpa_b, wpb_b, wout_b,
                   gfin, layer, final_norm=(layer == DEPTH - 1))
    return x2d.reshape(bsz, seq, dm)
```

```python
import functools
import math

import numpy as np
import jax
import jax.numpy as jnp
from jax import lax
from jax.experimental import pallas as pl
from jax.experimental.pallas import tpu as pltpu

D_MODEL = 1024
DEPTH = 4
CHUNK = 128
A_GROUPS = 8
N_HEADS = 16
HEAD_DIM = 64
DILATIONS = (1, 4, 16)
BLOCK = 128
EPS = 1e-6
NEG_INF = -1e30
N_COL_BLOCKS = 9
IN_COLS = N_COL_BLOCKS * D_MODEL

LANES = 128
MXU_WIDTH = 256
VMEM_LIMIT = 56 * 1024 * 1024

TM_IN = 1024
TM_MIX = 512

F32 = jnp.float32
BF16 = jnp.bfloat16


_LOG2E = math.log2(math.e)
_GELU_C1 = math.sqrt(2.0 / math.pi)
_GELU_C3 = _GELU_C1 * 0.044715
_Q_SCALE = HEAD_DIM ** -0.5 * _LOG2E


def _gelu(y):
    return (0.5 * y) * (1.0 + jnp.tanh(y * (_GELU_C1 + _GELU_C3 * (y * y))))


def _silu(y):
    h = 0.5 * y
    return h * (1.0 + jnp.tanh(h))


def _sigmoid(y):
    return 0.5 + 0.5 * jnp.tanh(0.5 * y)


def _inproj_kernel(x_ref, g_ref, w_ref, o_ref, h_ref, wb_ref):
    i = pl.program_id(0)
    j = pl.program_id(1)

    @pl.when(j == 0)
    def _():
        x = x_ref[...]
        ms = jnp.mean(x * x, axis=-1, keepdims=True)
        h_ref[...] = (x * lax.rsqrt(ms + EPS) * g_ref[...]).astype(BF16)

    @pl.when(i == 0)
    def _():
        wb_ref[j] = w_ref[...].astype(BF16)

    def project(act):
        for c in range(0, D_MODEL, MXU_WIDTH):
            cs = slice(c, c + MXU_WIDTH)
            y = jnp.dot(h_ref[...], wb_ref[j, :, cs], preferred_element_type=F32)
            o_ref[:, cs] = act(y).astype(BF16)

    @pl.when(j < 2)
    def _():
        project(_gelu)

    @pl.when((j == 2) | (j == 6))
    def _():
        project(_silu)

    @pl.when((j >= 3) & (j <= 5))
    def _():
        scale = jnp.where(j == 3, _Q_SCALE, 1.0).astype(F32)
        project(lambda y: y * scale)

    @pl.when(j >= 7)
    def _():
        project(_sigmoid)


def _inproj(x2d, g_all, w_all, layer):
    m = x2d.shape[0]
    grid = (m // TM_IN, N_COL_BLOCKS)
    last = N_COL_BLOCKS - 1

    def w_index(i, j):
        return (layer, 0, jnp.where(i == 0, j, last))

    return pl.pallas_call(
        _inproj_kernel,
        name="inproj",
        grid=grid,
        in_specs=[
            pl.BlockSpec((TM_IN, D_MODEL), lambda i, j: (i, 0)),
            pl.BlockSpec((None, 1, D_MODEL), lambda i, j: (layer, 0, 0)),
            pl.BlockSpec((None, D_MODEL, D_MODEL), w_index),
        ],
        out_specs=pl.BlockSpec((TM_IN, D_MODEL), lambda i, j: (i, j)),
        scratch_shapes=[pltpu.VMEM((TM_IN, D_MODEL), BF16),
                        pltpu.VMEM((N_COL_BLOCKS, D_MODEL, D_MODEL), BF16)],
        out_shape=jax.ShapeDtypeStruct((m, IN_COLS), BF16),
        compiler_params=pltpu.CompilerParams(
            dimension_semantics=("arbitrary", "arbitrary"),
            vmem_limit_bytes=VMEM_LIMIT),
    )(x2d, g_all, w_all)


def _attn_bias_tables():
    slopes = 2.0 ** (-8.0 * np.arange(1, N_HEADS + 1, dtype=np.float64) / N_HEADS)
    qpos = np.arange(BLOCK)[:, None] + BLOCK
    kpos = np.arange(2 * BLOCK)[None, :]
    dist = qpos - kpos
    valid = (dist >= 0) & (dist <= BLOCK)
    out = np.zeros((N_HEADS // 2, len(DILATIONS), 2, 2 * BLOCK, 2 * BLOCK), np.float32)
    for hp in range(N_HEADS // 2):
        for p, dil in enumerate(DILATIONS):
            for h in range(2):
                pen = -slopes[2 * hp + h] * (dist * dil) * _LOG2E
                rows = slice(h * BLOCK, (h + 1) * BLOCK)
                out[hp, p, 0, rows] = np.where(valid, pen, NEG_INF)
                out[hp, p, 1, rows] = np.where(valid & (kpos >= BLOCK), pen, NEG_INF)
    return out


def _attn_kernel(q_ref, k_ref, v_ref, bias_ref, o_ref,
                 qf, kf, vf, q4, k4, v4, q16, k16, v16,
                 acc2, l2, m2, acc3g, l3g, m3g, acc3, l3, m3):
    seq = q_ref.shape[1]
    d2, d3 = DILATIONS[1], DILATIONS[2]
    assert d3 == d2 * d2
    n4 = seq // d2

    qf[...] = q_ref[0].astype(F32)
    kf[...] = k_ref[0].astype(F32)
    vf[...] = v_ref[0].astype(F32)
    for src, dst in ((qf, q4), (kf, k4), (vf, v4)):
        for r in range(d2):
            dst[r * n4:(r + 1) * n4, :] = src[pl.ds(r, n4, stride=d2), :]
    for src, dst in ((q4, q16), (k4, k16), (v4, v16)):
        for a in range(d2):
            for b in range(d2):
                c0 = (a * d2 + b) * BLOCK
                dst[c0:c0 + BLOCK, :] = src[pl.ds(a * n4 + b, BLOCK, stride=d2), :].astype(BF16)

    lane = lax.broadcasted_iota(jnp.int32, (BLOCK, LANES), 1)
    head0 = lane < HEAD_DIM
    ones_cols = jnp.ones((2 * BLOCK, LANES), BF16)

    def block(p, src, g0, first):
        q_src, k_src, v_src = src
        q = q_src[g0:g0 + BLOCK, :]
        if g0 == 0:
            k = jnp.concatenate([k_src[:BLOCK, :]] * 2, axis=0)
            v = jnp.concatenate([v_src[:BLOCK, :]] * 2, axis=0)
        else:
            k = k_src[g0 - BLOCK:g0 + BLOCK, :]
            v = v_src[g0 - BLOCK:g0 + BLOCK, :]
        zero = jnp.zeros_like(q)
        q2 = jnp.concatenate([jnp.where(head0, q, zero), jnp.where(head0, zero, q)],
                             axis=0).astype(BF16)
        s = lax.dot_general(q2, k.astype(BF16), (((1,), (1,)), ((), ())),
                            preferred_element_type=F32)
        s = s + bias_ref[0, p, int(first)]
        m = jnp.max(s, axis=-1, keepdims=True)
        pr = jnp.exp2(s - m).astype(BF16)
        vext = jnp.concatenate([v.astype(BF16), ones_cols], axis=1)
        r = jnp.dot(pr, vext, preferred_element_type=F32)
        acc = jnp.where(head0, r[:BLOCK, :LANES], r[BLOCK:, :LANES])
        l = jnp.where(head0, r[:BLOCK, LANES:], r[BLOCK:, LANES:])
        mb = jnp.where(head0, jnp.broadcast_to(m[:BLOCK], (BLOCK, LANES)),
                       jnp.broadcast_to(m[BLOCK:], (BLOCK, LANES)))
        return acc, l, mb


    for n in range(n4 // BLOCK):
        for r in range(d2):
            vals = block(1, (q4, k4, v4), r * n4 + n * BLOCK, n == 0)
            for ref, val in zip((acc2, l2, m2), vals):
                ref[pl.ds(r + d2 * BLOCK * n, BLOCK, stride=d2), :] = val

    for a in range(d2):
        for b in range(d2):
            vals = block(2, (q16, k16, v16), (a * d2 + b) * BLOCK, True)
            for ref, val in zip((acc3g, l3g, m3g), vals):
                ref[pl.ds(a * n4 + b, BLOCK, stride=d2), :] = val
    for src, dst in ((acc3g, acc3), (l3g, l3), (m3g, m3)):
        for r in range(d2):
            dst[pl.ds(r, n4, stride=d2), :] = src[r * n4:(r + 1) * n4, :]

    for n in range(seq // BLOCK):
        qs = n * BLOCK
        acc1, l1, m1 = block(0, (q_ref.at[0], k_ref.at[0], v_ref.at[0]), qs, n == 0)
        rows = slice(qs, qs + BLOCK)
        mm2, mm3 = m2[rows, :], m3[rows, :]
        mx = jnp.maximum(jnp.maximum(m1, mm2), mm3)
        e1 = jnp.exp2(m1 - mx)
        e2 = jnp.exp2(mm2 - mx)
        e3 = jnp.exp2(mm3 - mx)
        num = e1 * acc1 + e2 * acc2[rows, :] + e3 * acc3[rows, :]
        den = e1 * l1 + e2 * l2[rows, :] + e3 * l3[rows, :]
        o_ref[0, rows, :] = (num / den).astype(o_ref.dtype)


def _attn(proj3d, bias):
    bsz, seq, _ = proj3d.shape
    n_pairs = N_HEADS // 2
    col0 = 3 * D_MODEL // LANES
    per = D_MODEL // LANES
    blk = (1, seq, LANES)
    stat = pltpu.VMEM((seq, LANES), F32)
    return pl.pallas_call(
        _attn_kernel,
        name="attn",
        grid=(n_pairs, bsz),
        in_specs=[
            pl.BlockSpec(blk, lambda hp, b: (b, 0, col0 + hp)),
            pl.BlockSpec(blk, lambda hp, b: (b, 0, col0 + per + hp)),
            pl.BlockSpec(blk, lambda hp, b: (b, 0, col0 + 2 * per + hp)),
            pl.BlockSpec((1, len(DILATIONS), 2, 2 * BLOCK, 2 * BLOCK),
                         lambda hp, b: (hp, 0, 0, 0, 0)),
        ],
        out_specs=pl.BlockSpec(blk, lambda hp, b: (b, 0, hp)),
        out_shape=jax.ShapeDtypeStruct((bsz, seq, D_MODEL), BF16),
        scratch_shapes=[stat] * 6 + [pltpu.VMEM((seq, LANES), BF16)] * 3 + [stat] * 9,
        compiler_params=pltpu.CompilerParams(
            dimension_semantics=("arbitrary", "arbitrary"),
            vmem_limit_bytes=VMEM_LIMIT),
    )(proj3d, proj3d, proj3d, bias)


def _mix_kernel(u_ref, gv_ref, gate_ref, att_ref, bgate_ref, sa_ref, sb_ref, x_ref,
                ws_ref, bs_ref, gvw_ref, wpa_ref, wpb_ref, wout_ref, gfin_ref,
                out_ref, ya_ref, *, final_norm):
    tm = x_ref.shape[0]
    n_chunks = tm // CHUNK
    yb = (att_ref[...].astype(F32) * bgate_ref[...].astype(F32)).astype(BF16)
    zb = jnp.dot(yb, wpb_ref[...], preferred_element_type=F32)

    gv = gv_ref[...].astype(F32)
    ms = jnp.mean(gv * gv, axis=-1, keepdims=True)
    vn = (gv * lax.rsqrt(ms + EPS) * gvw_ref[...]).astype(BF16)

    row = lax.broadcasted_iota(jnp.int32, (CHUNK, CHUNK), 0)
    col = lax.broadcasted_iota(jnp.int32, (CHUNK, CHUNK), 1)
    causal = col <= row
    for g in range(A_GROUPS):
        w = jnp.where(causal, ws_ref[g], jnp.zeros((), BF16))
        cs = slice(g * CHUNK, (g + 1) * CHUNK)
        vg = jnp.concatenate([vn[c * CHUNK:(c + 1) * CHUNK, cs] for c in range(n_chunks)], axis=1)
        mixed_all = jnp.dot(w, vg, preferred_element_type=F32)
        for c in range(n_chunks):
            rs = slice(c * CHUNK, (c + 1) * CHUNK)
            mixed = mixed_all[:, c * CHUNK:(c + 1) * CHUNK] + bs_ref[:, cs]
            ya = u_ref[rs, cs].astype(F32) * mixed * gate_ref[rs, cs].astype(F32)
            ya_ref[rs, cs] = ya.astype(BF16)

    za = jnp.dot(ya_ref[...], wpa_ref[...], preferred_element_type=F32)
    merged = (sa_ref[...].astype(F32) * za + sb_ref[...].astype(F32) * zb).astype(BF16)
    xn = x_ref[...] + jnp.dot(merged, wout_ref[...], preferred_element_type=F32)
    if final_norm:
        ms2 = jnp.mean(xn * xn, axis=-1, keepdims=True)
        xn = xn * lax.rsqrt(ms2 + EPS) * gfin_ref[...]
    out_ref[...] = xn


def _mix(proj, att2d, x2d, ws_bf16, bs_full, gvw, wpa, wpb, wout, gfin, layer, final_norm):
    m = x2d.shape[0]
    tile = (TM_MIX, D_MODEL)

    def col_block(jc):
        return pl.BlockSpec(tile, lambda i: (i, jc))

    def whole(shape):
        return pl.BlockSpec((None,) + shape, lambda i: (layer,) + (0,) * len(shape))

    return pl.pallas_call(
        functools.partial(_mix_kernel, final_norm=final_norm),
        name="mix",
        grid=(m // TM_MIX,),
        in_specs=[
            col_block(0), col_block(1), col_block(2),
            col_block(0),
            col_block(6), col_block(7), col_block(8),
            col_block(0),
            whole((A_GROUPS, CHUNK, CHUNK)),
            whole((CHUNK, D_MODEL)),
            whole((1, D_MODEL)),
            whole((D_MODEL, D_MODEL)), whole((D_MODEL, D_MODEL)), whole((D_MODEL, D_MODEL)),
            pl.BlockSpec((1, D_MODEL), lambda i: (0, 0)),
        ],
        out_specs=col_block(0),
        out_shape=jax.ShapeDtypeStruct((m, D_MODEL), F32),
        scratch_shapes=[pltpu.VMEM(tile, BF16)],
        compiler_params=pltpu.CompilerParams(
            dimension_semantics=("arbitrary",),
            vmem_limit_bytes=VMEM_LIMIT),
    )(proj, proj, proj, att2d, proj, proj, proj, x2d,
      ws_bf16, bs_full, gvw, wpa, wpb, wout, gfin)


def kernel(x, g_norm, w_in, w_s, b_s, g_v, w_proj_a, w_proj_b, w_out, g_final):
    bsz, seq, dm = x.shape
    assert dm == D_MODEL and seq % (DILATIONS[-1] * BLOCK) == 0
    assert (bsz * seq) % TM_IN == 0 and (bsz * seq) % TM_MIX == 0
    m = bsz * seq
    x2d = x.reshape(m, dm)
    bias = jnp.asarray(_attn_bias_tables())
    w_s_b = w_s.astype(BF16)
    wpa_b = w_proj_a.astype(BF16)
    wpb_b = w_proj_b.astype(BF16)
    wout_b = w_out.astype(BF16)
    bs_full = jnp.repeat(jnp.swapaxes(b_s, 1, 2), CHUNK, axis=2)
    gfin = g_final.reshape(1, dm)
    g_norm3 = g_norm.reshape(DEPTH, 1, dm)
    g_v3 = g_v.reshape(DEPTH, 1, dm)
    for layer in range(DEPTH):
        proj = _inproj(x2d, g_norm3, w_in, layer)
        att = _attn(proj.reshape(bsz, seq, IN_COLS), bias)
        x2d = _mix(proj, att.reshape(m, dm), x2d, w_s_b, bs_full, g_v3, wpa_b, wpb_b, wout_b,
                   gfin, layer, final_norm=(layer == DEPTH - 1))
    return x2d.reshape(bsz, seq, dm)
```

```python
import functools
import math

import numpy as np
import jax
import jax.numpy as jnp
from jax import lax
from jax.experimental import pallas as pl
from jax.experimental.pallas import tpu as pltpu

D_MODEL = 1024
DEPTH = 4
CHUNK = 128
A_GROUPS = 8
N_HEADS = 16
HEAD_DIM = 64
DILATIONS = (1, 4, 16)
BLOCK = 128
EPS = 1e-6
NEG_INF = -1e30
N_COL_BLOCKS = 9
IN_COLS = N_COL_BLOCKS * D_MODEL

LANES = 128
MXU_WIDTH = 256
VMEM_LIMIT = 56 * 1024 * 1024

TM_IN = 1024
TM_MIX = 512

F32 = jnp.float32
BF16 = jnp.bfloat16


_LOG2E = math.log2(math.e)
_GELU_C1 = math.sqrt(2.0 / math.pi)
_GELU_C3 = _GELU_C1 * 0.044715
_Q_SCALE = HEAD_DIM ** -0.5 * _LOG2E


def _gelu(y):
    return (0.5 * y) * (1.0 + jnp.tanh(y * (_GELU_C1 + _GELU_C3 * (y * y))))


def _silu(y):
    h = 0.5 * y
    return h * (1.0 + jnp.tanh(h))


def _sigmoid(y):
    return 0.5 + 0.5 * jnp.tanh(0.5 * y)


def _inproj_kernel(x_ref, g_ref, w_ref, o_ref, h_ref, wb_ref):
    i = pl.program_id(0)
    j = pl.program_id(1)

    @pl.when(i == 0)
    def _():
        wb_ref[j] = w_ref[...].astype(BF16)

    def project(act):
        for c in range(0, D_MODEL, MXU_WIDTH):
            cs = slice(c, c + MXU_WIDTH)
            y = jnp.dot(h_ref[...], wb_ref[j, :, cs], preferred_element_type=F32)
            o_ref[:, cs] = act(y).astype(BF16)

    @pl.when(j == 0)
    def _():
        x = x_ref[...]
        ms = jnp.mean(x * x, axis=-1, keepdims=True)
        h_ref[...] = (x * lax.rsqrt(ms + EPS) * g_ref[...]).astype(BF16)
        project(_gelu)

    @pl.when(j == 1)
    def _():
        project(_gelu)

    @pl.when((j == 2) | (j == 6))
    def _():
        project(_silu)

    @pl.when((j >= 3) & (j <= 5))
    def _():
        scale = jnp.where(j == 3, _Q_SCALE, 1.0).astype(F32)
        project(lambda y: y * scale)

    @pl.when(j >= 7)
    def _():
        project(_sigmoid)


def _inproj(x2d, g_all, w_all, layer):
    m = x2d.shape[0]
    grid = (m // TM_IN, N_COL_BLOCKS)
    last = N_COL_BLOCKS - 1

    def w_index(i, j):
        return (layer, 0, jnp.where(i == 0, j, last))

    return pl.pallas_call(
        _inproj_kernel,
        name="inproj",
        grid=grid,
        in_specs=[
            pl.BlockSpec((TM_IN, D_MODEL), lambda i, j: (i, 0)),
            pl.BlockSpec((None, 1, D_MODEL), lambda i, j: (layer, 0, 0)),
            pl.BlockSpec((None, D_MODEL, D_MODEL), w_index),
        ],
        out_specs=pl.BlockSpec((TM_IN, D_MODEL), lambda i, j: (i, j)),
        scratch_shapes=[pltpu.VMEM((TM_IN, D_MODEL), BF16),
                        pltpu.VMEM((N_COL_BLOCKS, D_MODEL, D_MODEL), BF16)],
        out_shape=jax.ShapeDtypeStruct((m, IN_COLS), BF16),
        compiler_params=pltpu.CompilerParams(
            dimension_semantics=("arbitrary", "arbitrary"),
            vmem_limit_bytes=VMEM_LIMIT),
    )(x2d, g_all, w_all)


def _attn_bias_tables():
    slopes = 2.0 ** (-8.0 * np.arange(1, N_HEADS + 1, dtype=np.float64) / N_HEADS)
    qpos = np.arange(BLOCK)[:, None] + BLOCK
    kpos = np.arange(2 * BLOCK)[None, :]
    dist = qpos - kpos
    valid = (dist >= 0) & (dist <= BLOCK)
    out = np.zeros((N_HEADS // 2, len(DILATIONS), 2, 2 * BLOCK, 2 * BLOCK), np.float32)
    for hp in range(N_HEADS // 2):
        for p, dil in enumerate(DILATIONS):
            for h in range(2):
                pen = -slopes[2 * hp + h] * (dist * dil) * _LOG2E
                rows = slice(h * BLOCK, (h + 1) * BLOCK)
                out[hp, p, 0, rows] = np.where(valid, pen, NEG_INF)
                out[hp, p, 1, rows] = np.where(valid & (kpos >= BLOCK), pen, NEG_INF)
    return out


def _attn_kernel(q_ref, k_ref, v_ref, bias_ref, o_ref,
                 qf, kf, vf, q4, k4, v4, q16, k16, v16,
                 acc2, l2, m2, acc3g, l3g, m3g, acc3, l3, m3):
    seq = q_ref.shape[1]
    d2, d3 = DILATIONS[1], DILATIONS[2]
    assert d3 == d2 * d2
    n4 = seq // d2

    qf[...] = q_ref[0].astype(F32)
    kf[...] = k_ref[0].astype(F32)
    vf[...] = v_ref[0].astype(F32)
    for src, dst in ((qf, q4), (kf, k4), (vf, v4)):
        for r in range(d2):
            dst[r * n4:(r + 1) * n4, :] = src[pl.ds(r, n4, stride=d2), :]
    for src, dst in ((q4, q16), (k4, k16), (v4, v16)):
        for a in range(d2):
            for b in range(d2):
                c0 = (a * d2 + b) * BLOCK
                dst[c0:c0 + BLOCK, :] = src[pl.ds(a * n4 + b, BLOCK, stride=d2), :].astype(BF16)

    lane = lax.broadcasted_iota(jnp.int32, (BLOCK, LANES), 1)
    head0 = lane < HEAD_DIM
    ones_cols = jnp.ones((2 * BLOCK, LANES), BF16)

    def block(p, src, g0, first):
        q_src, k_src, v_src = src
        q = q_src[g0:g0 + BLOCK, :]
        if g0 == 0:
            k = jnp.concatenate([k_src[:BLOCK, :]] * 2, axis=0)
            v = jnp.concatenate([v_src[:BLOCK, :]] * 2, axis=0)
        else:
            k = k_src[g0 - BLOCK:g0 + BLOCK, :]
            v = v_src[g0 - BLOCK:g0 + BLOCK, :]
        zero = jnp.zeros_like(q)
        q2 = jnp.concatenate([jnp.where(head0, q, zero), jnp.where(head0, zero, q)],
                             axis=0).astype(BF16)
        s = lax.dot_general(q2, k.astype(BF16), (((1,), (1,)), ((), ())),
                            preferred_element_type=F32)
        s = s + bias_ref[0, p, int(first)]
        m = jnp.max(s, axis=-1, keepdims=True)
        pr = jnp.exp2(s - m).astype(BF16)
        vext = jnp.concatenate([v.astype(BF16), ones_cols], axis=1)
        r = jnp.dot(pr, vext, preferred_element_type=F32)
        acc = jnp.where(head0, r[:BLOCK, :LANES], r[BLOCK:, :LANES])
        l = jnp.where(head0, r[:BLOCK, LANES:], r[BLOCK:, LANES:])
        mb = jnp.where(head0, jnp.broadcast_to(m[:BLOCK], (BLOCK, LANES)),
                       jnp.broadcast_to(m[BLOCK:], (BLOCK, LANES)))
        return acc, l, mb


    for n in range(n4 // BLOCK):
        for r in range(d2):
            vals = block(1, (q4, k4, v4), r * n4 + n * BLOCK, n == 0)
            for ref, val in zip((acc2, l2, m2), vals):
                ref[pl.ds(r + d2 * BLOCK * n, BLOCK, stride=d2), :] = val

    for a in range(d2):
        for b in range(d2):
            vals = block(2, (q16, k16, v16), (a * d2 + b) * BLOCK, True)
            for ref, val in zip((acc3g, l3g, m3g), vals):
                ref[pl.ds(a * n4 + b, BLOCK, stride=d2), :] = val
    for src, dst in ((acc3g, acc3), (l3g, l3), (m3g, m3)):
        for r in range(d2):
            dst[pl.ds(r, n4, stride=d2), :] = src[r * n4:(r + 1) * n4, :]

    for n in range(seq // BLOCK):
        qs = n * BLOCK
        acc1, l1, m1 = block(0, (q_ref.at[0], k_ref.at[0], v_ref.at[0]), qs, n == 0)
        rows = slice(qs, qs + BLOCK)
        mm2, mm3 = m2[rows, :], m3[rows, :]
        mx = jnp.maximum(jnp.maximum(m1, mm2), mm3)
        e1 = jnp.exp2(m1 - mx)
        e2 = jnp.exp2(mm2 - mx)
        e3 = jnp.exp2(mm3 - mx)
        num = e1 * acc1 + e2 * acc2[rows, :] + e3 * acc3[rows, :]
        den = e1 * l1 + e2 * l2[rows, :] + e3 * l3[rows, :]
        o_ref[0, rows, :] = (num / den).astype(o_ref.dtype)


def _attn(proj3d, bias):
    bsz, seq, _ = proj3d.shape
    n_pairs = N_HEADS // 2
    col0 = 3 * D_MODEL // LANES
    per = D_MODEL // LANES
    blk = (1, seq, LANES)
    stat = pltpu.VMEM((seq, LANES), F32)
    return pl.pallas_call(
        _attn_kernel,
        name="attn",
        grid=(n_pairs, bsz),
        in_specs=[
            pl.BlockSpec(blk, lambda hp, b: (b, 0, col0 + hp)),
            pl.BlockSpec(blk, lambda hp, b: (b, 0, col0 + per + hp)),
            pl.BlockSpec(blk, lambda hp, b: (b, 0, col0 + 2 * per + hp)),
            pl.BlockSpec((1, len(DILATIONS), 2, 2 * BLOCK, 2 * BLOCK),
                         lambda hp, b: (hp, 0, 0, 0, 0)),
        ],
        out_specs=pl.BlockSpec(blk, lambda hp, b: (b, 0, hp)),
        out_shape=jax.ShapeDtypeStruct((bsz, seq, D_MODEL), BF16),
        scratch_shapes=[stat] * 6 + [pltpu.VMEM((seq, LANES), BF16)] * 3 + [stat] * 9,
        compiler_params=pltpu.CompilerParams(
            dimension_semantics=("arbitrary", "arbitrary"),
            vmem_limit_bytes=VMEM_LIMIT),
    )(proj3d, proj3d, proj3d, bias)


def _mix_kernel(u_ref, gv_ref, gate_ref, att_ref, bgate_ref, sa_ref, sb_ref, x_ref,
                ws_ref, bs_ref, gvw_ref, wpa_ref, wpb_ref, wout_ref, gfin_ref,
                out_ref, ya_ref, *, final_norm):
    tm = x_ref.shape[0]
    n_chunks = tm // CHUNK
    yb = (att_ref[...].astype(F32) * bgate_ref[...].astype(F32)).astype(BF16)
    zb = jnp.dot(yb, wpb_ref[...], preferred_element_type=F32)

    gv = gv_ref[...].astype(F32)
    ms = jnp.mean(gv * gv, axis=-1, keepdims=True)
    vn = (gv * lax.rsqrt(ms + EPS) * gvw_ref[...]).astype(BF16)

    row = lax.broadcasted_iota(jnp.int32, (CHUNK, CHUNK), 0)
    col = lax.broadcasted_iota(jnp.int32, (CHUNK, CHUNK), 1)
    causal = col <= row
    for g in range(A_GROUPS):
        w = jnp.where(causal, ws_ref[g], jnp.zeros((), BF16))
        cs = slice(g * CHUNK, (g + 1) * CHUNK)
        vg = jnp.concatenate([vn[c * CHUNK:(c + 1) * CHUNK, cs] for c in range(n_chunks)], axis=1)
        mixed_all = jnp.dot(w, vg, preferred_element_type=F32)
        for c in range(n_chunks):
            rs = slice(c * CHUNK, (c + 1) * CHUNK)
            mixed = mixed_all[:, c * CHUNK:(c + 1) * CHUNK] + bs_ref[:, cs]
            ya = u_ref[rs, cs].astype(F32) * mixed * gate_ref[rs, cs].astype(F32)
            ya_ref[rs, cs] = ya.astype(BF16)

    za = jnp.dot(ya_ref[...], wpa_ref[...], preferred_element_type=F32)
    merged = (sa_ref[...].astype(F32) * za + sb_ref[...].astype(F32) * zb).astype(BF16)
    xn = x_ref[...] + jnp.dot(merged, wout_ref[...], preferred_element_type=F32)
    if final_norm:
        ms2 = jnp.mean(xn * xn, axis=-1, keepdims=True)
        xn = xn * lax.rsqrt(ms2 + EPS) * gfin_ref[...]
    out_ref[...] = xn


def _mix(proj, att2d, x2d, ws_bf16, bs_full, gvw, wpa, wpb, wout, gfin, layer, final_norm):
    m = x2d.shape[0]
    tile = (TM_MIX, D_MODEL)

    def col_block(jc):
        return pl.BlockSpec(tile, lambda i: (i, jc))

    def whole(shape):
        return pl.BlockSpec((None,) + shape, lambda i: (layer,) + (0,) * len(shape))

    return pl.pallas_call(
        functools.partial(_mix_kernel, final_norm=final_norm),
        name="mix",
        grid=(m // TM_MIX,),
        in_specs=[
            col_block(0), col_block(1), col_block(2),
            col_block(0),
            col_block(6), col_block(7), col_block(8),
            col_block(0),
            whole((A_GROUPS, CHUNK, CHUNK)),
            whole((CHUNK, D_MODEL)),
            whole((1, D_MODEL)),
            whole((D_MODEL, D_MODEL)), whole((D_MODEL, D_MODEL)), whole((D_MODEL, D_MODEL)),
            pl.BlockSpec((1, D_MODEL), lambda i: (0, 0)),
        ],
        out_specs=col_block(0),
        out_shape=jax.ShapeDtypeStruct((m, D_MODEL), F32),
        scratch_shapes=[pltpu.VMEM(tile, BF16)],
        compiler_params=pltpu.CompilerParams(
            dimension_semantics=("arbitrary",),
            vmem_limit_bytes=VMEM_LIMIT),
    )(proj, proj, proj, att2d, proj, proj, proj, x2d,
      ws_bf16, bs_full, gvw, wpa, wpb, wout, gfin)


def kernel(x, g_norm, w_in, w_s, b_s, g_v, w_proj_a, w_proj_b, w_out, g_final):
    bsz, seq, dm = x.shape
    assert dm == D_MODEL and seq % (DILATIONS[-1] * BLOCK) == 0
    assert (bsz * seq) % TM_IN == 0 and (bsz * seq) % TM_MIX == 0
    m = bsz * seq
    x2d = x.reshape(m, dm)
    bias = jnp.asarray(_attn_bias_tables())
    w_s_b = w_s.astype(BF16)
    wpa_b = w_proj_a.astype(BF16)
    wpb_b = w_proj_b.astype(BF16)
    wout_b = w_out.astype(BF16)
    bs_full = jnp.repeat(jnp.swapaxes(b_s, 1, 2), CHUNK, axis=2)
    gfin = g_final.reshape(1, dm)
    g_norm3 = g_norm.reshape(DEPTH, 1, dm)
    g_v3 = g_v.reshape(DEPTH, 1, dm)
    for layer in range(DEPTH):
        proj = _inproj(x2d, g_norm3, w_in, layer)
        att = _attn(proj.reshape(bsz, seq, IN_COLS), bias)
        x2d = _mix(proj, att.reshape(m, dm), x2d, w_s_b, bs_full, g_v3, wpa_b, wpb_b, wout_b,
                   gfin, layer, final_norm=(layer == DEPTH - 1))
    return x2d.reshape(bsz, seq, dm)
```

```python
import functools
import math

import numpy as np
import jax
import jax.numpy as jnp
from jax import lax
from jax.experimental import pallas as pl
from jax.experimental.pallas import tpu as pltpu

D_MODEL = 1024
DEPTH = 4
CHUNK = 128
A_GROUPS = 8
N_HEADS = 16
HEAD_DIM = 64
DILATIONS = (1, 4, 16)
BLOCK = 128
EPS = 1e-6
NEG_INF = -1e30
N_COL_BLOCKS = 9
IN_COLS = N_COL_BLOCKS * D_MODEL

LANES = 128
MXU_WIDTH = 256
VMEM_LIMIT = 56 * 1024 * 1024

TM_IN = 1024
TM_MIX = 512

F32 = jnp.float32
BF16 = jnp.bfloat16


_LOG2E = math.log2(math.e)
_GELU_C1 = math.sqrt(2.0 / math.pi)
_GELU_C3 = _GELU_C1 * 0.044715
_Q_SCALE = HEAD_DIM ** -0.5 * _LOG2E


def _gelu_half(h):
    return h + h * jnp.tanh(h * (2.0 * _GELU_C1 + 8.0 * _GELU_C3 * (h * h)))


def _silu_half(h):
    return h + h * jnp.tanh(h)


def _sigmoid_half(h):
    return 0.5 + 0.5 * jnp.tanh(h)


def _inproj_kernel(x_ref, g_ref, w_ref, o_ref, h_ref, wb_ref):
    i = pl.program_id(0)
    j = pl.program_id(1)

    @pl.when(i == 0)
    def _():
        halved = (j < 3) | (j > 5)
        wb_ref[j] = (w_ref[...] * jnp.where(halved, 0.5, 1.0).astype(F32)).astype(BF16)

    def project(act):
        for c in range(0, D_MODEL, MXU_WIDTH):
            cs = slice(c, c + MXU_WIDTH)
            y = jnp.dot(h_ref[...], wb_ref[j, :, cs], preferred_element_type=F32)
            o_ref[:, cs] = act(y).astype(BF16)

    @pl.when(j == 0)
    def _():
        x = x_ref[...]
        ms = jnp.mean(x * x, axis=-1, keepdims=True)
        h_ref[...] = (x * lax.rsqrt(ms + EPS) * g_ref[...]).astype(BF16)
        project(_gelu_half)

    @pl.when(j == 1)
    def _():
        project(_gelu_half)

    @pl.when((j == 2) | (j == 6))
    def _():
        project(_silu_half)

    @pl.when((j >= 3) & (j <= 5))
    def _():
        scale = jnp.where(j == 3, _Q_SCALE, 1.0).astype(F32)
        project(lambda y: y * scale)

    @pl.when(j >= 7)
    def _():
        project(_sigmoid_half)


def _inproj(x2d, g_all, w_all, layer):
    m = x2d.shape[0]
    grid = (m // TM_IN, N_COL_BLOCKS)
    last = N_COL_BLOCKS - 1

    def w_index(i, j):
        return (layer, 0, jnp.where(i == 0, j, last))

    return pl.pallas_call(
        _inproj_kernel,
        name="inproj",
        grid=grid,
        in_specs=[
            pl.BlockSpec((TM_IN, D_MODEL), lambda i, j: (i, 0)),
            pl.BlockSpec((None, 1, D_MODEL), lambda i, j: (layer, 0, 0)),
            pl.BlockSpec((None, D_MODEL, D_MODEL), w_index),
        ],
        out_specs=pl.BlockSpec((TM_IN, D_MODEL), lambda i, j: (i, j)),
        scratch_shapes=[pltpu.VMEM((TM_IN, D_MODEL), BF16),
                        pltpu.VMEM((N_COL_BLOCKS, D_MODEL, D_MODEL), BF16)],
        out_shape=jax.ShapeDtypeStruct((m, IN_COLS), BF16),
        compiler_params=pltpu.CompilerParams(
            dimension_semantics=("arbitrary", "arbitrary"),
            vmem_limit_bytes=VMEM_LIMIT),
    )(x2d, g_all, w_all)


def _attn_bias_tables():
    slopes = 2.0 ** (-8.0 * np.arange(1, N_HEADS + 1, dtype=np.float64) / N_HEADS)
    qpos = np.arange(BLOCK)[:, None] + BLOCK
    kpos = np.arange(2 * BLOCK)[None, :]
    dist = qpos - kpos
    valid = (dist >= 0) & (dist <= BLOCK)
    out = np.zeros((N_HEADS // 2, len(DILATIONS), 2, 2 * BLOCK, 2 * BLOCK), np.float32)
    for hp in range(N_HEADS // 2):
        for p, dil in enumerate(DILATIONS):
            for h in range(2):
                pen = -slopes[2 * hp + h] * (dist * dil) * _LOG2E
                rows = slice(h * BLOCK, (h + 1) * BLOCK)
                out[hp, p, 0, rows] = np.where(valid, pen, NEG_INF)
                out[hp, p, 1, rows] = np.where(valid & (kpos >= BLOCK), pen, NEG_INF)
    return out


def _attn_kernel(q_ref, k_ref, v_ref, bias_ref, o_ref,
                 qf, kf, vf, q4, k4, v4, q16, k16, v16,
                 acc2, l2, m2, acc3g, l3g, m3g, acc3, l3, m3):
    seq = q_ref.shape[1]
    d2, d3 = DILATIONS[1], DILATIONS[2]
    assert d3 == d2 * d2
    n4 = seq // d2

    qf[...] = q_ref[0].astype(F32)
    kf[...] = k_ref[0].astype(F32)
    vf[...] = v_ref[0].astype(F32)
    for src, dst in ((qf, q4), (kf, k4), (vf, v4)):
        for r in range(d2):
            dst[r * n4:(r + 1) * n4, :] = src[pl.ds(r, n4, stride=d2), :]
    for src, dst in ((q4, q16), (k4, k16), (v4, v16)):
        for a in range(d2):
            for b in range(d2):
                c0 = (a * d2 + b) * BLOCK
                dst[c0:c0 + BLOCK, :] = src[pl.ds(a * n4 + b, BLOCK, stride=d2), :].astype(BF16)

    lane = lax.broadcasted_iota(jnp.int32, (BLOCK, LANES), 1)
    head0 = lane < HEAD_DIM
    ones_cols = jnp.ones((2 * BLOCK, LANES), BF16)

    def block(p, src, g0, first):
        q_src, k_src, v_src = src
        q = q_src[g0:g0 + BLOCK, :]
        if g0 == 0:
            k = jnp.concatenate([k_src[:BLOCK, :]] * 2, axis=0)
            v = jnp.concatenate([v_src[:BLOCK, :]] * 2, axis=0)
        else:
            k = k_src[g0 - BLOCK:g0 + BLOCK, :]
            v = v_src[g0 - BLOCK:g0 + BLOCK, :]
        zero = jnp.zeros_like(q)
        q2 = jnp.concatenate([jnp.where(head0, q, zero), jnp.where(head0, zero, q)],
                             axis=0).astype(BF16)
        s = lax.dot_general(q2, k.astype(BF16), (((1,), (1,)), ((), ())),
                            preferred_element_type=F32)
        s = s + bias_ref[0, p, int(first)]
        m = jnp.max(s, axis=-1, keepdims=True)
        pr = jnp.exp2(s - m).astype(BF16)
        vext = jnp.concatenate([v.astype(BF16), ones_cols], axis=1)
        r = jnp.dot(pr, vext, preferred_element_type=F32)
        acc = jnp.where(head0, r[:BLOCK, :LANES], r[BLOCK:, :LANES])
        l = jnp.where(head0, r[:BLOCK, LANES:], r[BLOCK:, LANES:])
        mb = jnp.where(head0, jnp.broadcast_to(m[:BLOCK], (BLOCK, LANES)),
                       jnp.broadcast_to(m[BLOCK:], (BLOCK, LANES)))
        return acc, l, mb


    for n in range(n4 // BLOCK):
        for r in range(d2):
            vals = block(1, (q4, k4, v4), r * n4 + n * BLOCK, n == 0)
            for ref, val in zip((acc2, l2, m2), vals):
                ref[pl.ds(r + d2 * BLOCK * n, BLOCK, stride=d2), :] = val

    for a in range(d2):
        for b in range(d2):
            vals = block(2, (q16, k16, v16), (a * d2 + b) * BLOCK, True)
            for ref, val in zip((acc3g, l3g, m3g), vals):
                ref[pl.ds(a * n4 + b, BLOCK, stride=d2), :] = val
    for src, dst in ((acc3g, acc3), (l3g, l3), (m3g, m3)):
        for r in range(d2):
            dst[pl.ds(r, n4, stride=d2), :] = src[r * n4:(r + 1) * n4, :]

    for n in range(seq // BLOCK):
        qs = n * BLOCK
        acc1, l1, m1 = block(0, (q_ref.at[0], k_ref.at[0], v_ref.at[0]), qs, n == 0)
        rows = slice(qs, qs + BLOCK)
        mm2, mm3 = m2[rows, :], m3[rows, :]
        mx = jnp.maximum(jnp.maximum(m1, mm2), mm3)
        e1 = jnp.exp2(m1 - mx)
        e2 = jnp.exp2(mm2 - mx)
        e3 = jnp.exp2(mm3 - mx)
        num = e1 * acc1 + e2 * acc2[rows, :] + e3 * acc3[rows, :]
        den = e1 * l1 + e2 * l2[rows, :] + e3 * l3[rows, :]
        o_ref[0, rows, :] = (num / den).astype(o_ref.dtype)


def _attn(proj3d, bias):
    bsz, seq, _ = proj3d.shape
    n_pairs = N_HEADS // 2
    col0 = 3 * D_MODEL // LANES
    per = D_MODEL // LANES
    blk = (1, seq, LANES)
    stat = pltpu.VMEM((seq, LANES), F32)
    return pl.pallas_call(
        _attn_kernel,
        name="attn",
        grid=(n_pairs, bsz),
        in_specs=[
            pl.BlockSpec(blk, lambda hp, b: (b, 0, col0 + hp)),
            pl.BlockSpec(blk, lambda hp, b: (b, 0, col0 + per + hp)),
            pl.BlockSpec(blk, lambda hp, b: (b, 0, col0 + 2 * per + hp)),
            pl.BlockSpec((1, len(DILATIONS), 2, 2 * BLOCK, 2 * BLOCK),
                         lambda hp, b: (hp, 0, 0, 0, 0)),
        ],
        out_specs=pl.BlockSpec(blk, lambda hp, b: (b, 0, hp)),
        out_shape=jax.ShapeDtypeStruct((bsz, seq, D_MODEL), BF16),
        scratch_shapes=[stat] * 6 + [pltpu.VMEM((seq, LANES), BF16)] * 3 + [stat] * 9,
        compiler_params=pltpu.CompilerParams(
            dimension_semantics=("arbitrary", "arbitrary"),
            vmem_limit_bytes=VMEM_LIMIT),
    )(proj3d, proj3d, proj3d, bias)


def _mix_kernel(u_ref, gv_ref, gate_ref, att_ref, bgate_ref, sa_ref, sb_ref, x_ref,
                ws_ref, bs_ref, gvw_ref, wpa_ref, wpb_ref, wout_ref, gfin_ref,
                out_ref, ya_ref, *, final_norm):
    tm = x_ref.shape[0]
    n_chunks = tm // CHUNK
    yb = (att_ref[...].astype(F32) * bgate_ref[...].astype(F32)).astype(BF16)
    zb = jnp.dot(yb, wpb_ref[...], preferred_element_type=F32)

    gv = gv_ref[...].astype(F32)
    ms = jnp.mean(gv * gv, axis=-1, keepdims=True)
    vn = (gv * lax.rsqrt(ms + EPS) * gvw_ref[...]).astype(BF16)

    row = lax.broadcasted_iota(jnp.int32, (CHUNK, CHUNK), 0)
    col = lax.broadcasted_iota(jnp.int32, (CHUNK, CHUNK), 1)
    causal = col <= row
    for g in range(A_GROUPS):
        w = jnp.where(causal, ws_ref[g], jnp.zeros((), BF16))
        cs = slice(g * CHUNK, (g + 1) * CHUNK)
        vg = jnp.concatenate([vn[c * CHUNK:(c + 1) * CHUNK, cs] for c in range(n_chunks)], axis=1)
        mixed_all = jnp.dot(w, vg, preferred_element_type=F32)
        for c in range(n_chunks):
            rs = slice(c * CHUNK, (c + 1) * CHUNK)
            mixed = mixed_all[:, c * CHUNK:(c + 1) * CHUNK] + bs_ref[:, cs]
            ya = u_ref[rs, cs].astype(F32) * mixed * gate_ref[rs, cs].astype(F32)
            ya_ref[rs, cs] = ya.astype(BF16)

    za = jnp.dot(ya_ref[...], wpa_ref[...], preferred_element_type=F32)
    merged = (sa_ref[...].astype(F32) * za + sb_ref[...].astype(F32) * zb).astype(BF16)
    xn = x_ref[...] + jnp.dot(merged, wout_ref[...], preferred_element_type=F32)
    if final_norm:
        ms2 = jnp.mean(xn * xn, axis=-1, keepdims=True)
        xn = xn * lax.rsqrt(ms2 + EPS) * gfin_ref[...]
    out_ref[...] = xn


def _mix(proj, att2d, x2d, ws_bf16, bs_full, gvw, wpa, wpb, wout, gfin, layer, final_norm):
    m = x2d.shape[0]
    tile = (TM_MIX, D_MODEL)

    def col_block(jc):
        return pl.BlockSpec(tile, lambda i: (i, jc))

    def whole(shape):
        return pl.BlockSpec((None,) + shape, lambda i: (layer,) + (0,) * len(shape))

    return pl.pallas_call(
        functools.partial(_mix_kernel, final_norm=final_norm),
        name="mix",
        grid=(m // TM_MIX,),
        in_specs=[
            col_block(0), col_block(1), col_block(2),
            col_block(0),
            col_block(6), col_block(7), col_block(8),
            col_block(0),
            whole((A_GROUPS, CHUNK, CHUNK)),
            whole((CHUNK, D_MODEL)),
            whole((1, D_MODEL)),
            whole((D_MODEL, D_MODEL)), whole((D_MODEL, D_MODEL)), whole((D_MODEL, D_MODEL)),
            pl.BlockSpec((1, D_MODEL), lambda i: (0, 0)),
        ],
        out_specs=col_block(0),
        out_shape=jax.ShapeDtypeStruct((m, D_MODEL), F32),
        scratch_shapes=[pltpu.VMEM(tile, BF16)],
        compiler_params=pltpu.CompilerParams(
            dimension_semantics=("arbitrary",),
            vmem_limit_bytes=VMEM_LIMIT),
    )(proj, proj, proj, att2d, proj, proj, proj, x2d,
      ws_bf16, bs_full, gvw, wpa, wpb, wout, gfin)


def kernel(x, g_norm, w_in, w_s, b_s, g_v, w_proj_a, w_proj_b, w_out, g_final):
    bsz, seq, dm = x.shape
    assert dm == D_MODEL and seq % (DILATIONS[-1] * BLOCK) == 0
    assert (bsz * seq) % TM_IN == 0 and (bsz * seq) % TM_MIX == 0
    m = bsz * seq
    x2d = x.reshape(m, dm)
    bias = jnp.asarray(_attn_bias_tables())
    w_s_b = w_s.astype(BF16)
    wpa_b = w_proj_a.astype(BF16)
    wpb_b = w_proj_b.astype(BF16)
    wout_b = w_out.astype(BF16)
    bs_full = jnp.repeat(jnp.swapaxes(b_s, 1, 2), CHUNK, axis=2)
    gfin = g_final.reshape(1, dm)
    g_norm3 = g_norm.reshape(DEPTH, 1, dm)
    g_v3 = g_v.reshape(DEPTH, 1, dm)
    for layer in range(DEPTH):
        proj = _inproj(x2d, g_norm3, w_in, layer)
        att = _attn(proj.reshape(bsz, seq, IN_COLS), bias)
        x2d = _mix(proj, att.reshape(m, dm), x2d, w_s_b, bs_full, g_v3, wpa_b, wpb_b, wout_b,
                   gfin, layer, final_norm=(layer == DEPTH - 1))
    return x2d.reshape(bsz, seq, dm)
```

```python
import functools
import math

import numpy as np
import jax
import jax.numpy as jnp
from jax import lax
from jax.experimental import pallas as pl
from jax.experimental.pallas import tpu as pltpu

D_MODEL = 1024
DEPTH = 4
CHUNK = 128
A_GROUPS = 8
N_HEADS = 16
HEAD_DIM = 64
DILATIONS = (1, 4, 16)
BLOCK = 128
EPS = 1e-6
NEG_INF = -1e30
N_COL_BLOCKS = 9
IN_COLS = N_COL_BLOCKS * D_MODEL

LANES = 128
MXU_WIDTH = 256
VMEM_LIMIT = 56 * 1024 * 1024

TM_IN = 1024
TM_MIX = 512

F32 = jnp.float32
BF16 = jnp.bfloat16


_LOG2E = math.log2(math.e)
_GELU_C1 = math.sqrt(2.0 / math.pi)
_GELU_C3 = _GELU_C1 * 0.044715
_Q_SCALE = HEAD_DIM ** -0.5 * _LOG2E


def _gelu_half(h):
    return h + h * jnp.tanh(h * (2.0 * _GELU_C1 + 8.0 * _GELU_C3 * (h * h)))


def _silu_half(h):
    return h + h * jnp.tanh(h)


def _sigmoid_half(h):
    return 0.5 + 0.5 * jnp.tanh(h)


def _identity(y):
    return y


_STEP_ACTS = ((_gelu_half, _gelu_half, _silu_half),
              (lambda y: y * _Q_SCALE, _identity, _identity),
              (_silu_half, _sigmoid_half, _sigmoid_half))
_STEP_COLS = len(_STEP_ACTS[0]) * D_MODEL


def _inproj_kernel(x_ref, g_ref, w_ref, o_ref, h_ref, wb_ref):
    i = pl.program_id(0)
    j = pl.program_id(1)

    @pl.when(i == 0)
    def _():
        wb_ref[j] = (w_ref[...] * jnp.where(j == 1, 1.0, 0.5).astype(F32)).astype(BF16)

    def project(acts):
        for b, act in enumerate(acts):
            for c in range(b * D_MODEL, (b + 1) * D_MODEL, MXU_WIDTH):
                cs = slice(c, c + MXU_WIDTH)
                y = jnp.dot(h_ref[...], wb_ref[j, :, cs], preferred_element_type=F32)
                o_ref[:, cs] = act(y).astype(BF16)

    @pl.when(j == 0)
    def _():
        x = x_ref[...]
        ms = jnp.mean(x * x, axis=-1, keepdims=True)
        h_ref[...] = (x * lax.rsqrt(ms + EPS) * g_ref[...]).astype(BF16)
        project(_STEP_ACTS[0])

    for step in range(1, len(_STEP_ACTS)):
        @pl.when(j == step)
        def _(step=step):
            project(_STEP_ACTS[step])


def _inproj(x2d, g_all, w_all, layer):
    m = x2d.shape[0]
    n_steps = len(_STEP_ACTS)
    grid = (m // TM_IN, n_steps)

    def w_index(i, j):
        return (layer, 0, jnp.where(i == 0, j, n_steps - 1))

    return pl.pallas_call(
        _inproj_kernel,
        name="inproj",
        grid=grid,
        in_specs=[
            pl.BlockSpec((TM_IN, D_MODEL), lambda i, j: (i, 0)),
            pl.BlockSpec((None, 1, D_MODEL), lambda i, j: (layer, 0, 0)),
            pl.BlockSpec((None, D_MODEL, _STEP_COLS), w_index, pipeline_mode=pl.Buffered(1)),
        ],
        out_specs=pl.BlockSpec((TM_IN, _STEP_COLS), lambda i, j: (i, j)),
        scratch_shapes=[pltpu.VMEM((TM_IN, D_MODEL), BF16),
                        pltpu.VMEM((n_steps, D_MODEL, _STEP_COLS), BF16)],
        out_shape=jax.ShapeDtypeStruct((m, IN_COLS), BF16),
        compiler_params=pltpu.CompilerParams(
            dimension_semantics=("arbitrary", "arbitrary"),
            vmem_limit_bytes=VMEM_LIMIT),
    )(x2d, g_all, w_all)


def _attn_bias_tables():
    slopes = 2.0 ** (-8.0 * np.arange(1, N_HEADS + 1, dtype=np.float64) / N_HEADS)
    qpos = np.arange(BLOCK)[:, None] + BLOCK
    kpos = np.arange(2 * BLOCK)[None, :]
    dist = qpos - kpos
    valid = (dist >= 0) & (dist <= BLOCK)
    out = np.zeros((N_HEADS // 2, len(DILATIONS), 2, 2 * BLOCK, 2 * BLOCK), np.float32)
    for hp in range(N_HEADS // 2):
        for p, dil in enumerate(DILATIONS):
            for h in range(2):
                pen = -slopes[2 * hp + h] * (dist * dil) * _LOG2E
                rows = slice(h * BLOCK, (h + 1) * BLOCK)
                out[hp, p, 0, rows] = np.where(valid, pen, NEG_INF)
                out[hp, p, 1, rows] = np.where(valid & (kpos >= BLOCK), pen, NEG_INF)
    return out


def _attn_kernel(q_ref, k_ref, v_ref, bias_ref, o_ref,
                 qf, kf, vf, q4, k4, v4, q16, k16, v16,
                 acc2, l2, m2, acc3g, l3g, m3g, acc3, l3, m3):
    seq = q_ref.shape[1]
    d2, d3 = DILATIONS[1], DILATIONS[2]
    assert d3 == d2 * d2
    n4 = seq // d2

    qf[...] = q_ref[0].astype(F32)
    kf[...] = k_ref[0].astype(F32)
    vf[...] = v_ref[0].astype(F32)
    for src, dst in ((qf, q4), (kf, k4), (vf, v4)):
        for r in range(d2):
            dst[r * n4:(r + 1) * n4, :] = src[pl.ds(r, n4, stride=d2), :]
    for src, dst in ((q4, q16), (k4, k16), (v4, v16)):
        for a in range(d2):
            for b in range(d2):
                c0 = (a * d2 + b) * BLOCK
                dst[c0:c0 + BLOCK, :] = src[pl.ds(a * n4 + b, BLOCK, stride=d2), :].astype(BF16)

    lane = lax.broadcasted_iota(jnp.int32, (BLOCK, LANES), 1)
    head0 = lane < HEAD_DIM
    ones_cols = jnp.ones((2 * BLOCK, LANES), BF16)

    def block(p, src, g0, first):
        q_src, k_src, v_src = src
        q = q_src[g0:g0 + BLOCK, :]
        if g0 == 0:
            k = jnp.concatenate([k_src[:BLOCK, :]] * 2, axis=0)
            v = jnp.concatenate([v_src[:BLOCK, :]] * 2, axis=0)
        else:
            k = k_src[g0 - BLOCK:g0 + BLOCK, :]
            v = v_src[g0 - BLOCK:g0 + BLOCK, :]
        zero = jnp.zeros_like(q)
        q2 = jnp.concatenate([jnp.where(head0, q, zero), jnp.where(head0, zero, q)],
                             axis=0).astype(BF16)
        s = lax.dot_general(q2, k.astype(BF16), (((1,), (1,)), ((), ())),
                            preferred_element_type=F32)
        s = s + bias_ref[0, p, int(first)]
        m = jnp.max(s, axis=-1, keepdims=True)
        pr = jnp.exp2(s - m).astype(BF16)
        vext = jnp.concatenate([v.astype(BF16), ones_cols], axis=1)
        r = jnp.dot(pr, vext, preferred_element_type=F32)
        acc = jnp.where(head0, r[:BLOCK, :LANES], r[BLOCK:, :LANES])
        l = jnp.where(head0, r[:BLOCK, LANES:], r[BLOCK:, LANES:])
        mb = jnp.where(head0, jnp.broadcast_to(m[:BLOCK], (BLOCK, LANES)),
                       jnp.broadcast_to(m[BLOCK:], (BLOCK, LANES)))
        return acc, l, mb


    for n in range(n4 // BLOCK):
        for r in range(d2):
            vals = block(1, (q4, k4, v4), r * n4 + n * BLOCK, n == 0)
            for ref, val in zip((acc2, l2, m2), vals):
                ref[pl.ds(r + d2 * BLOCK * n, BLOCK, stride=d2), :] = val

    for a in range(d2):
        for b in range(d2):
            vals = block(2, (q16, k16, v16), (a * d2 + b) * BLOCK, True)
            for ref, val in zip((acc3g, l3g, m3g), vals):
                ref[pl.ds(a * n4 + b, BLOCK, stride=d2), :] = val
    for src, dst in ((acc3g, acc3), (l3g, l3), (m3g, m3)):
        for r in range(d2):
            dst[pl.ds(r, n4, stride=d2), :] = src[r * n4:(r + 1) * n4, :]

    for n in range(seq // BLOCK):
        qs = n * BLOCK
        acc1, l1, m1 = block(0, (q_ref.at[0], k_ref.at[0], v_ref.at[0]), qs, n == 0)
        rows = slice(qs, qs + BLOCK)
        mm2, mm3 = m2[rows, :], m3[rows, :]
        mx = jnp.maximum(jnp.maximum(m1, mm2), mm3)
        e1 = jnp.exp2(m1 - mx)
        e2 = jnp.exp2(mm2 - mx)
        e3 = jnp.exp2(mm3 - mx)
        num = e1 * acc1 + e2 * acc2[rows, :] + e3 * acc3[rows, :]
        den = e1 * l1 + e2 * l2[rows, :] + e3 * l3[rows, :]
        o_ref[0, rows, :] = (num / den).astype(o_ref.dtype)


def _attn(proj3d, bias):
    bsz, seq, _ = proj3d.shape
    n_pairs = N_HEADS // 2
    col0 = 3 * D_MODEL // LANES
    per = D_MODEL // LANES
    blk = (1, seq, LANES)
    stat = pltpu.VMEM((seq, LANES), F32)
    return pl.pallas_call(
        _attn_kernel,
        name="attn",
        grid=(n_pairs, bsz),
        in_specs=[
            pl.BlockSpec(blk, lambda hp, b: (b, 0, col0 + hp)),
            pl.BlockSpec(blk, lambda hp, b: (b, 0, col0 + per + hp)),
            pl.BlockSpec(blk, lambda hp, b: (b, 0, col0 + 2 * per + hp)),
            pl.BlockSpec((1, len(DILATIONS), 2, 2 * BLOCK, 2 * BLOCK),
                         lambda hp, b: (hp, 0, 0, 0, 0)),
        ],
        out_specs=pl.BlockSpec(blk, lambda hp, b: (b, 0, hp)),
        out_shape=jax.ShapeDtypeStruct((bsz, seq, D_MODEL), BF16),
        scratch_shapes=[stat] * 6 + [pltpu.VMEM((seq, LANES), BF16)] * 3 + [stat] * 9,
        compiler_params=pltpu.CompilerParams(
            dimension_semantics=("arbitrary", "arbitrary"),
            vmem_limit_bytes=VMEM_LIMIT),
    )(proj3d, proj3d, proj3d, bias)


def _mix_kernel(u_ref, gv_ref, gate_ref, att_ref, bgate_ref, sa_ref, sb_ref, x_ref,
                ws_ref, bs_ref, gvw_ref, wpa_ref, wpb_ref, wout_ref, gfin_ref,
                out_ref, ya_ref, *, final_norm):
    tm = x_ref.shape[0]
    n_chunks = tm // CHUNK
    yb = (att_ref[...].astype(F32) * bgate_ref[...].astype(F32)).astype(BF16)
    zb = jnp.dot(yb, wpb_ref[...], preferred_element_type=F32)

    gv = gv_ref[...].astype(F32)
    ms = jnp.mean(gv * gv, axis=-1, keepdims=True)
    vn = (gv * lax.rsqrt(ms + EPS) * gvw_ref[...]).astype(BF16)

    row = lax.broadcasted_iota(jnp.int32, (CHUNK, CHUNK), 0)
    col = lax.broadcasted_iota(jnp.int32, (CHUNK, CHUNK), 1)
    causal = col <= row
    for g in range(A_GROUPS):
        w = jnp.where(causal, ws_ref[g], jnp.zeros((), BF16))
        cs = slice(g * CHUNK, (g + 1) * CHUNK)
        vg = jnp.concatenate([vn[c * CHUNK:(c + 1) * CHUNK, cs] for c in range(n_chunks)], axis=1)
        mixed_all = jnp.dot(w, vg, preferred_element_type=F32)
        for c in range(n_chunks):
            rs = slice(c * CHUNK, (c + 1) * CHUNK)
            mixed = mixed_all[:, c * CHUNK:(c + 1) * CHUNK] + bs_ref[:, cs]
            ya = u_ref[rs, cs].astype(F32) * mixed * gate_ref[rs, cs].astype(F32)
            ya_ref[rs, cs] = ya.astype(BF16)

    za = jnp.dot(ya_ref[...], wpa_ref[...], preferred_element_type=F32)
    merged = (sa_ref[...].astype(F32) * za + sb_ref[...].astype(F32) * zb).astype(BF16)
    xn = x_ref[...] + jnp.dot(merged, wout_ref[...], preferred_element_type=F32)
    if final_norm:
        ms2 = jnp.mean(xn * xn, axis=-1, keepdims=True)
        xn = xn * lax.rsqrt(ms2 + EPS) * gfin_ref[...]
    out_ref[...] = xn


def _mix(proj, att2d, x2d, ws_bf16, bs_full, gvw, wpa, wpb, wout, gfin, layer, final_norm):
    m = x2d.shape[0]
    tile = (TM_MIX, D_MODEL)

    def col_block(jc):
        return pl.BlockSpec(tile, lambda i: (i, jc))

    def whole(shape):
        return pl.BlockSpec((None,) + shape, lambda i: (layer,) + (0,) * len(shape))

    return pl.pallas_call(
        functools.partial(_mix_kernel, final_norm=final_norm),
        name="mix",
        grid=(m // TM_MIX,),
        in_specs=[
            col_block(0), col_block(1), col_block(2),
            col_block(0),
            col_block(6), col_block(7), col_block(8),
            col_block(0),
            whole((A_GROUPS, CHUNK, CHUNK)),
            whole((CHUNK, D_MODEL)),
            whole((1, D_MODEL)),
            whole((D_MODEL, D_MODEL)), whole((D_MODEL, D_MODEL)), whole((D_MODEL, D_MODEL)),
            pl.BlockSpec((1, D_MODEL), lambda i: (0, 0)),
        ],
        out_specs=col_block(0),
        out_shape=jax.ShapeDtypeStruct((m, D_MODEL), F32),
        scratch_shapes=[pltpu.VMEM(tile, BF16)],
        compiler_params=pltpu.CompilerParams(
            dimension_semantics=("arbitrary",),
            vmem_limit_bytes=VMEM_LIMIT),
    )(proj, proj, proj, att2d, proj, proj, proj, x2d,
      ws_bf16, bs_full, gvw, wpa, wpb, wout, gfin)


def kernel(x, g_norm, w_in, w_s, b_s, g_v, w_proj_a, w_proj_b, w_out, g_final):
    bsz, seq, dm = x.shape
    assert dm == D_MODEL and seq % (DILATIONS[-1] * BLOCK) == 0
    assert (bsz * seq) % TM_IN == 0 and (bsz * seq) % TM_MIX == 0
    m = bsz * seq
    x2d = x.reshape(m, dm)
    bias = jnp.asarray(_attn_bias_tables())
    w_s_b = w_s.astype(BF16)
    wpa_b = w_proj_a.astype(BF16)
    wpb_b = w_proj_b.astype(BF16)
    wout_b = w_out.astype(BF16)
    bs_full = jnp.repeat(jnp.swapaxes(b_s, 1, 2), CHUNK, axis=2)
    gfin = g_final.reshape(1, dm)
    g_norm3 = g_norm.reshape(DEPTH, 1, dm)
    g_v3 = g_v.reshape(DEPTH, 1, dm)
    for layer in range(DEPTH):
        proj = _inproj(x2d, g_norm3, w_in, layer)
        att = _attn(proj.reshape(bsz, seq, IN_COLS), bias)
        x2d = _mix(proj, att.reshape(m, dm), x2d, w_s_b, bs_full, g_v3, wpa_b, wpb_b, wout_b,
                   gfin, layer, final_norm=(layer == DEPTH - 1))
    return x2d.reshape(bsz, seq, dm)
```

```python
import functools
import math

import numpy as np
import jax
import jax.numpy as jnp
from jax import lax
from jax.experimental import pallas as pl
from jax.experimental.pallas import tpu as pltpu

D_MODEL = 1024
DEPTH = 4
CHUNK = 128
A_GROUPS = 8
N_HEADS = 16
HEAD_DIM = 64
DILATIONS = (1, 4, 16)
BLOCK = 128
EPS = 1e-6
NEG_INF = -1e30
N_COL_BLOCKS = 9
IN_COLS = N_COL_BLOCKS * D_MODEL

LANES = 128
MXU_WIDTH = 256
VMEM_LIMIT = 56 * 1024 * 1024

TM_IN = 1024
TM_MIX = 512

F32 = jnp.float32
BF16 = jnp.bfloat16


_LOG2E = math.log2(math.e)
_GELU_C1 = math.sqrt(2.0 / math.pi)
_GELU_C3 = _GELU_C1 * 0.044715
_Q_SCALE = HEAD_DIM ** -0.5 * _LOG2E


def _gelu_half(h):
    return h + h * jnp.tanh(h * (2.0 * _GELU_C1 + 8.0 * _GELU_C3 * (h * h)))


def _silu_half(h):
    return h + h * jnp.tanh(h)


def _sigmoid_half(h):
    return 0.5 + 0.5 * jnp.tanh(h)


def _identity(y):
    return y


_STEP_ACTS = ((_gelu_half, _gelu_half, _silu_half),
              (lambda y: y * _Q_SCALE, _identity, _identity),
              (_silu_half, _sigmoid_half, _sigmoid_half))
_STEP_WEIGHT_SCALE = (0.5, 1.0, 0.5)
_STEP_COLS = len(_STEP_ACTS[0]) * D_MODEL


def _inproj_kernel(x_ref, g_ref, w_ref, o_ref, h_ref, wb_ref):
    i = pl.program_id(0)
    j = pl.program_id(1)

    def step_body(step, first_tile):
        if step == 0:
            x = x_ref[...]
            ms = jnp.mean(x * x, axis=-1, keepdims=True)
            h_ref[...] = (x * lax.rsqrt(ms + EPS) * g_ref[...]).astype(BF16)
        for b, act in enumerate(_STEP_ACTS[step]):
            for c in range(b * D_MODEL, (b + 1) * D_MODEL, MXU_WIDTH):
                cs = slice(c, c + MXU_WIDTH)
                if first_tile:
                    w = w_ref[:, cs]
                    if _STEP_WEIGHT_SCALE[step] != 1.0:
                        w = w * _STEP_WEIGHT_SCALE[step]
                    wb_ref[step, :, cs] = w.astype(BF16)
                y = jnp.dot(h_ref[...], wb_ref[step, :, cs], preferred_element_type=F32)
                o_ref[:, cs] = act(y).astype(BF16)

    for step in range(len(_STEP_ACTS)):
        @pl.when((j == step) & (i == 0))
        def _(step=step):
            step_body(step, True)

        @pl.when((j == step) & (i > 0))
        def _(step=step):
            step_body(step, False)


def _inproj(x2d, g_all, w_all, layer):
    m = x2d.shape[0]
    n_steps = len(_STEP_ACTS)
    grid = (m // TM_IN, n_steps)

    def w_index(i, j):
        return (layer, 0, jnp.where(i == 0, j, n_steps - 1))

    return pl.pallas_call(
        _inproj_kernel,
        name="inproj",
        grid=grid,
        in_specs=[
            pl.BlockSpec((TM_IN, D_MODEL), lambda i, j: (i, 0)),
            pl.BlockSpec((None, 1, D_MODEL), lambda i, j: (layer, 0, 0)),
            pl.BlockSpec((None, D_MODEL, _STEP_COLS), w_index, pipeline_mode=pl.Buffered(1)),
        ],
        out_specs=pl.BlockSpec((TM_IN, _STEP_COLS), lambda i, j: (i, j)),
        scratch_shapes=[pltpu.VMEM((TM_IN, D_MODEL), BF16),
                        pltpu.VMEM((n_steps, D_MODEL, _STEP_COLS), BF16)],
        out_shape=jax.ShapeDtypeStruct((m, IN_COLS), BF16),
        compiler_params=pltpu.CompilerParams(
            dimension_semantics=("arbitrary", "arbitrary"),
            vmem_limit_bytes=VMEM_LIMIT),
    )(x2d, g_all, w_all)


def _attn_bias_tables():
    slopes = 2.0 ** (-8.0 * np.arange(1, N_HEADS + 1, dtype=np.float64) / N_HEADS)
    qpos = np.arange(BLOCK)[:, None] + BLOCK
    kpos = np.arange(2 * BLOCK)[None, :]
    dist = qpos - kpos
    valid = (dist >= 0) & (dist <= BLOCK)
    out = np.zeros((N_HEADS // 2, len(DILATIONS), 2, 2 * BLOCK, 2 * BLOCK), np.float32)
    for hp in range(N_HEADS // 2):
        for p, dil in enumerate(DILATIONS):
            for h in range(2):
                pen = -slopes[2 * hp + h] * (dist * dil) * _LOG2E
                rows = slice(h * BLOCK, (h + 1) * BLOCK)
                out[hp, p, 0, rows] = np.where(valid, pen, NEG_INF)
                out[hp, p, 1, rows] = np.where(valid & (kpos >= BLOCK), pen, NEG_INF)
    return out


def _attn_kernel(q_ref, k_ref, v_ref, bias_ref, o_ref,
                 qf, kf, vf, q4, k4, v4, q16, k16, v16,
                 acc2, l2, m2, acc3g, l3g, m3g, acc3, l3, m3):
    seq = q_ref.shape[1]
    d2, d3 = DILATIONS[1], DILATIONS[2]
    assert d3 == d2 * d2
    n4 = seq // d2

    qf[...] = q_ref[0].astype(F32)
    kf[...] = k_ref[0].astype(F32)
    vf[...] = v_ref[0].astype(F32)
    for src, dst in ((qf, q4), (kf, k4), (vf, v4)):
        for r in range(d2):
            dst[r * n4:(r + 1) * n4, :] = src[pl.ds(r, n4, stride=d2), :]
    for src, dst in ((q4, q16), (k4, k16), (v4, v16)):
        for a in range(d2):
            for b in range(d2):
                c0 = (a * d2 + b) * BLOCK
                dst[c0:c0 + BLOCK, :] = src[pl.ds(a * n4 + b, BLOCK, stride=d2), :].astype(BF16)

    lane = lax.broadcasted_iota(jnp.int32, (BLOCK, LANES), 1)
    head0 = lane < HEAD_DIM
    ones_cols = jnp.ones((2 * BLOCK, LANES), BF16)

    def block(p, src, g0, first):
        q_src, k_src, v_src = src
        q = q_src[g0:g0 + BLOCK, :]
        if g0 == 0:
            k = jnp.concatenate([k_src[:BLOCK, :]] * 2, axis=0)
            v = jnp.concatenate([v_src[:BLOCK, :]] * 2, axis=0)
        else:
            k = k_src[g0 - BLOCK:g0 + BLOCK, :]
            v = v_src[g0 - BLOCK:g0 + BLOCK, :]
        zero = jnp.zeros_like(q)
        q2 = jnp.concatenate([jnp.where(head0, q, zero), jnp.where(head0, zero, q)],
                             axis=0).astype(BF16)
        s = lax.dot_general(q2, k.astype(BF16), (((1,), (1,)), ((), ())),
                            preferred_element_type=F32)
        s = s + bias_ref[0, p, int(first)]
        m = jnp.max(s, axis=-1, keepdims=True)
        pr = jnp.exp2(s - m).astype(BF16)
        vext = jnp.concatenate([v.astype(BF16), ones_cols], axis=1)
        r = jnp.dot(pr, vext, preferred_element_type=F32)
        acc = jnp.where(head0, r[:BLOCK, :LANES], r[BLOCK:, :LANES])
        l = jnp.where(head0, r[:BLOCK, LANES:], r[BLOCK:, LANES:])
        mb = jnp.where(head0, jnp.broadcast_to(m[:BLOCK], (BLOCK, LANES)),
                       jnp.broadcast_to(m[BLOCK:], (BLOCK, LANES)))
        return acc, l, mb


    for n in range(n4 // BLOCK):
        for r in range(d2):
            vals = block(1, (q4, k4, v4), r * n4 + n * BLOCK, n == 0)
            for ref, val in zip((acc2, l2, m2), vals):
                ref[pl.ds(r + d2 * BLOCK * n, BLOCK, stride=d2), :] = val

    for a in range(d2):
        for b in range(d2):
            vals = block(2, (q16, k16, v16), (a * d2 + b) * BLOCK, True)
            for ref, val in zip((acc3g, l3g, m3g), vals):
                ref[pl.ds(a * n4 + b, BLOCK, stride=d2), :] = val
    for src, dst in ((acc3g, acc3), (l3g, l3), (m3g, m3)):
        for r in range(d2):
            dst[pl.ds(r, n4, stride=d2), :] = src[r * n4:(r + 1) * n4, :]

    for n in range(seq // BLOCK):
        qs = n * BLOCK
        acc1, l1, m1 = block(0, (q_ref.at[0], k_ref.at[0], v_ref.at[0]), qs, n == 0)
        rows = slice(qs, qs + BLOCK)
        mm2, mm3 = m2[rows, :], m3[rows, :]
        mx = jnp.maximum(jnp.maximum(m1, mm2), mm3)
        e1 = jnp.exp2(m1 - mx)
        e2 = jnp.exp2(mm2 - mx)
        e3 = jnp.exp2(mm3 - mx)
        num = e1 * acc1 + e2 * acc2[rows, :] + e3 * acc3[rows, :]
        den = e1 * l1 + e2 * l2[rows, :] + e3 * l3[rows, :]
        o_ref[0, rows, :] = (num / den).astype(o_ref.dtype)


def _attn(proj3d, bias):
    bsz, seq, _ = proj3d.shape
    n_pairs = N_HEADS // 2
    col0 = 3 * D_MODEL // LANES
    per = D_MODEL // LANES
    blk = (1, seq, LANES)
    stat = pltpu.VMEM((seq, LANES), F32)
    return pl.pallas_call(
        _attn_kernel,
        name="attn",
        grid=(n_pairs, bsz),
        in_specs=[
            pl.BlockSpec(blk, lambda hp, b: (b, 0, col0 + hp)),
            pl.BlockSpec(blk, lambda hp, b: (b, 0, col0 + per + hp)),
            pl.BlockSpec(blk, lambda hp, b: (b, 0, col0 + 2 * per + hp)),
            pl.BlockSpec((1, len(DILATIONS), 2, 2 * BLOCK, 2 * BLOCK),
                         lambda hp, b: (hp, 0, 0, 0, 0)),
        ],
        out_specs=pl.BlockSpec(blk, lambda hp, b: (b, 0, hp)),
        out_shape=jax.ShapeDtypeStruct((bsz, seq, D_MODEL), BF16),
        scratch_shapes=[stat] * 6 + [pltpu.VMEM((seq, LANES), BF16)] * 3 + [stat] * 9,
        compiler_params=pltpu.CompilerParams(
            dimension_semantics=("arbitrary", "arbitrary"),
            vmem_limit_bytes=VMEM_LIMIT),
    )(proj3d, proj3d, proj3d, bias)


def _mix_kernel(u_ref, gv_ref, gate_ref, att_ref, bgate_ref, sa_ref, sb_ref, x_ref,
                ws_ref, bs_ref, gvw_ref, wpa_ref, wpb_ref, wout_ref, gfin_ref,
                out_ref, ya_ref, *, final_norm):
    tm = x_ref.shape[0]
    n_chunks = tm // CHUNK
    yb = (att_ref[...].astype(F32) * bgate_ref[...].astype(F32)).astype(BF16)
    zb = jnp.dot(yb, wpb_ref[...], preferred_element_type=F32)

    gv = gv_ref[...].astype(F32)
    ms = jnp.mean(gv * gv, axis=-1, keepdims=True)
    vn = (gv * lax.rsqrt(ms + EPS) * gvw_ref[...]).astype(BF16)

    row = lax.broadcasted_iota(jnp.int32, (CHUNK, CHUNK), 0)
    col = lax.broadcasted_iota(jnp.int32, (CHUNK, CHUNK), 1)
    causal = col <= row
    for g in range(A_GROUPS):
        w = jnp.where(causal, ws_ref[g], jnp.zeros((), BF16))
        cs = slice(g * CHUNK, (g + 1) * CHUNK)
        vg = jnp.concatenate([vn[c * CHUNK:(c + 1) * CHUNK, cs] for c in range(n_chunks)], axis=1)
        mixed_all = jnp.dot(w, vg, preferred_element_type=F32)
        for c in range(n_chunks):
            rs = slice(c * CHUNK, (c + 1) * CHUNK)
            mixed = mixed_all[:, c * CHUNK:(c + 1) * CHUNK] + bs_ref[:, cs]
            ya = u_ref[rs, cs].astype(F32) * mixed * gate_ref[rs, cs].astype(F32)
            ya_ref[rs, cs] = ya.astype(BF16)

    za = jnp.dot(ya_ref[...], wpa_ref[...], preferred_element_type=F32)
    merged = (sa_ref[...].astype(F32) * za + sb_ref[...].astype(F32) * zb).astype(BF16)
    xn = x_ref[...] + jnp.dot(merged, wout_ref[...], preferred_element_type=F32)
    if final_norm:
        ms2 = jnp.mean(xn * xn, axis=-1, keepdims=True)
        xn = xn * lax.rsqrt(ms2 + EPS) * gfin_ref[...]
    out_ref[...] = xn


def _mix(proj, att2d, x2d, ws_bf16, bs_full, gvw, wpa, wpb, wout, gfin, layer, final_norm):
    m = x2d.shape[0]
    tile = (TM_MIX, D_MODEL)

    def col_block(jc):
        return pl.BlockSpec(tile, lambda i: (i, jc))

    def whole(shape):
        return pl.BlockSpec((None,) + shape, lambda i: (layer,) + (0,) * len(shape))

    return pl.pallas_call(
        functools.partial(_mix_kernel, final_norm=final_norm),
        name="mix",
        grid=(m // TM_MIX,),
        in_specs=[
            col_block(0), col_block(1), col_block(2),
            col_block(0),
            col_block(6), col_block(7), col_block(8),
            col_block(0),
            whole((A_GROUPS, CHUNK, CHUNK)),
            whole((CHUNK, D_MODEL)),
            whole((1, D_MODEL)),
            whole((D_MODEL, D_MODEL)), whole((D_MODEL, D_MODEL)), whole((D_MODEL, D_MODEL)),
            pl.BlockSpec((1, D_MODEL), lambda i: (0, 0)),
        ],
        out_specs=col_block(0),
        out_shape=jax.ShapeDtypeStruct((m, D_MODEL), F32),
        scratch_shapes=[pltpu.VMEM(tile, BF16)],
        compiler_params=pltpu.CompilerParams(
            dimension_semantics=("arbitrary",),
            vmem_limit_bytes=VMEM_LIMIT),
    )(proj, proj, proj, att2d, proj, proj, proj, x2d,
      ws_bf16, bs_full, gvw, wpa, wpb, wout, gfin)


def kernel(x, g_norm, w_in, w_s, b_s, g_v, w_proj_a, w_proj_b, w_out, g_final):
    bsz, seq, dm = x.shape
    assert dm == D_MODEL and seq % (DILATIONS[-1] * BLOCK) == 0
    assert (bsz * seq) % TM_IN == 0 and (bsz * seq) % TM_MIX == 0
    m = bsz * seq
    x2d = x.reshape(m, dm)
    bias = jnp.asarray(_attn_bias_tables())
    w_s_b = w_s.astype(BF16)
    wpa_b = w_proj_a.astype(BF16)
    wpb_b = w_proj_b.astype(BF16)
    wout_b = w_out.astype(BF16)
    bs_full = jnp.repeat(jnp.swapaxes(b_s, 1, 2), CHUNK, axis=2)
    gfin = g_final.reshape(1, dm)
    g_norm3 = g_norm.reshape(DEPTH, 1, dm)
    g_v3 = g_v.reshape(DEPTH, 1, dm)
    for layer in range(DEPTH):
        proj = _inproj(x2d, g_norm3, w_in, layer)
        att = _attn(proj.reshape(bsz, seq, IN_COLS), bias)
        x2d = _mix(proj, att.reshape(m, dm), x2d, w_s_b, bs_full, g_v3, wpa_b, wpb_b, wout_b,
                   gfin, layer, final_norm=(layer == DEPTH - 1))
    return x2d.reshape(bsz, seq, dm)
```
